```python
import jax
import jax.numpy as jnp
from jax import lax
import numpy as np

D_MODEL = 1024
BATCH = 4
SEQ = 8192
DEPTH = 2

CHUNK = 64
EPS = 1e-6
CONV_W = 4
N_BRANCH = 4
BR_WIDTH = D_MODEL // 2

RG_BLOCKS = 8
RG_BLOCK = BR_WIDTH // RG_BLOCKS
RG_C = 8.0

SSD_HEAD_DIM = 64
SSD_HEADS = BR_WIDTH // SSD_HEAD_DIM
SSD_GROUPS = 2
SSD_HPG = SSD_HEADS // SSD_GROUPS
SSD_STATE = 128
SSD_XBC = BR_WIDTH + 2 * SSD_GROUPS * SSD_STATE

ML_HEADS = 4
ML_HEAD_DIM = BR_WIDTH // ML_HEADS

RET_HEADS = 4
RET_QK_DIM = 64
RET_V_DIM = BR_WIDTH // RET_HEADS
ROPE_BASE = 10000.0

IN_WIDTHS = (
    BR_WIDTH, BR_WIDTH,
    SSD_XBC, BR_WIDTH, SSD_HEADS,
    BR_WIDTH, BR_WIDTH, BR_WIDTH, BR_WIDTH, BR_WIDTH,
    ML_HEADS, ML_HEADS,
    RET_HEADS * RET_QK_DIM, RET_HEADS * RET_QK_DIM,
    BR_WIDTH, BR_WIDTH,
    N_BRANCH * D_MODEL,
)
W_IN = sum(IN_WIDTHS)
SPLIT_POINTS = tuple(int(v) for v in np.cumsum(IN_WIDTHS)[:-1])

kernel_name = 'hybrid_rglru_ssd_mlstm_retention_encoder'


def rmsnorm(x, g):
    xf = x.astype(jnp.float32)
    y = xf * lax.rsqrt(jnp.mean(xf * xf, axis=-1, keepdims=True) + EPS)
    return (y * g.astype(jnp.float32)).astype(x.dtype)


def headwise_rmsnorm(y, g, n_heads):
    b, s, w = y.shape
    yh = y.reshape(b, s, n_heads, w // n_heads).astype(jnp.float32)
    yh = yh * lax.rsqrt(jnp.mean(yh * yh, axis=-1, keepdims=True) + EPS)
    return yh.reshape(b, s, w) * g.astype(jnp.float32)


def causal_depthwise_conv(x, w, b):
    s = x.shape[1]
    xp = jnp.pad(x, ((0, 0), (CONV_W - 1, 0), (0, 0)))
    out = b + w[CONV_W - 1] * x
    for j in range(CONV_W - 1):
        out = out + w[j] * xp[:, j:j + s]
    return out


def causal_mask(n):
    return jnp.tril(jnp.ones((n, n), dtype=bool))


def segsum_exp(a_cs):
    diff = a_cs[..., :, None] - a_cs[..., None, :]
    return jnp.exp(jnp.where(causal_mask(a_cs.shape[-1]), diff, -jnp.inf))


def chunk_heads(t, n_heads):
    b, s, w = t.shape
    return t.reshape(b, s // CHUNK, CHUNK, n_heads, w // n_heads).transpose(0, 3, 1, 2, 4).astype(jnp.float32)


def chunk_gates(t, n_heads):
    b, s, _ = t.shape
    return t.reshape(b, s // CHUNK, CHUNK, n_heads).transpose(0, 3, 1, 2).astype(jnp.float32)


def rope(x, pos):
    half = x.shape[-1] // 2
    inv = ROPE_BASE ** (-jnp.arange(half, dtype=jnp.float32) / half)
    ang = pos.astype(jnp.float32)[:, None] * inv
    cos = jnp.cos(ang)[None, :, None, :]
    sin = jnp.sin(ang)[None, :, None, :]
    x1, x2 = x[..., :half], x[..., half:]
    return jnp.concatenate([x1 * cos - x2 * sin, x1 * sin + x2 * cos], axis=-1)


def rglru_branch(u, z, conv_w, conv_b, wa, ba, wx, bx, lam):
    bsz, s, _ = u.shape
    xc = causal_depthwise_conv(u, conv_w, conv_b)
    xb = xc.reshape(bsz, s, RG_BLOCKS, RG_BLOCK)
    r = jax.nn.sigmoid(jnp.einsum('bshi,hij->bshj', xb, wa).reshape(bsz, s, BR_WIDTH) + ba)
    i = jax.nn.sigmoid(jnp.einsum('bshi,hij->bshj', xb, wx).reshape(bsz, s, BR_WIDTH) + bx)
    log_a = -RG_C * r.astype(jnp.float32) * jax.nn.softplus(-lam.astype(jnp.float32))
    a = jnp.exp(log_a)
    drive = jnp.sqrt(-jnp.expm1(2.0 * log_a)) * (i * xc).astype(jnp.float32)

    def combine(left, right):
        a1, b1 = left
        a2, b2 = right
        return a1 * a2, a2 * b1 + b2

    _, h = lax.associative_scan(combine, (a, drive), axis=1)
    return h * jax.nn.silu(z.astype(jnp.float32))


def ssd_branch(xbc_raw, z, dt_raw, conv_w, conv_b, dt_bias, a_log, d_skip, norm_g):
    bsz, s, _ = z.shape
    nc = s // CHUNK
    xbc = jax.nn.silu(causal_depthwise_conv(xbc_raw, conv_w, conv_b)).astype(jnp.float32)
    gn = SSD_GROUPS * SSD_STATE
    xh = xbc[..., :BR_WIDTH].reshape(bsz, nc, CHUNK, SSD_GROUPS, SSD_HPG, SSD_HEAD_DIM)
    bm = xbc[..., BR_WIDTH:BR_WIDTH + gn].reshape(bsz, nc, CHUNK, SSD_GROUPS, SSD_STATE)
    cm = xbc[..., BR_WIDTH + gn:].reshape(bsz, nc, CHUNK, SSD_GROUPS, SSD_STATE)
    dt = jax.nn.softplus(dt_raw.astype(jnp.float32) + dt_bias.astype(jnp.float32))
    a_dt = -jnp.exp(a_log.astype(jnp.float32)) * dt
    x_dt = xh * dt.reshape(bsz, nc, CHUNK, SSD_GROUPS, SSD_HPG)[..., None]
    a_cs = jnp.cumsum(a_dt.reshape(bsz, nc, CHUNK, SSD_GROUPS, SSD_HPG).transpose(0, 3, 4, 1, 2), axis=-1)
    cb = jnp.einsum('bclgn,bcsgn->bgcls', cm, bm)
    y_diag = jnp.einsum('bgcls,bgecls,bcsgep->bclgep', cb, segsum_exp(a_cs), x_dt)
    decay_to_end = jnp.exp(a_cs[..., -1:] - a_cs)
    states = jnp.einsum('bclgn,bgecl,bclgep->bcgepn', bm, decay_to_end, x_dt)
    chunk_decay = jnp.exp(a_cs[..., -1])

    def step(h, inp):
        dec, st = inp
        return dec[..., None, None] * h + st, h

    h0 = jnp.zeros((bsz, SSD_GROUPS, SSD_HPG, SSD_HEAD_DIM, SSD_STATE), jnp.float32)
    _, prev = lax.scan(step, h0, (jnp.moveaxis(chunk_decay, -1, 0), jnp.moveaxis(states, 1, 0)))
    y_off = jnp.einsum('bclgn,cbgepn,bgecl->bclgep', cm, prev, jnp.exp(a_cs))
    y = y_diag + y_off + d_skip.astype(jnp.float32).reshape(SSD_GROUPS, SSD_HPG)[:, :, None] * xh
    y = y.reshape(bsz, s, BR_WIDTH)
    return rmsnorm(y * jax.nn.silu(z.astype(jnp.float32)), norm_g)


def mlstm_branch(q, k, v, o_raw, z, i_raw, f_raw, i_bias, f_bias, norm_g):
    bsz, s, _ = q.shape
    qh = chunk_heads(q, ML_HEADS)
    kh = chunk_heads(k, ML_HEADS) * (ML_HEAD_DIM ** -0.5)
    vh = chunk_heads(v, ML_HEADS)
    log_i = chunk_gates(i_raw, ML_HEADS) + i_bias.astype(jnp.float32)[None, :, None, None]
    log_f = jax.nn.log_sigmoid(chunk_gates(f_raw, ML_HEADS) + f_bias.astype(jnp.float32)[None, :, None, None])
    f_cs = jnp.cumsum(log_f, axis=-1)
    f_tot = f_cs[..., -1]
    w_end = f_tot[..., None] - f_cs + log_i
    m_loc = jnp.max(w_end, axis=-1)
    p_end = jnp.exp(w_end - m_loc[..., None])
    c_loc = jnp.einsum('bhcl,bhcld,bhcle->bhcde', p_end, vh, kh)
    n_loc = jnp.einsum('bhcl,bhcle->bhce', p_end, kh)

    def step(carry, inp):
        c_st, n_st, m_st = carry
        ft, ml, cl, nl = inp
        m_new = jnp.maximum(ft + m_st, ml)
        s_old = jnp.exp(ft + m_st - m_new)
        s_loc = jnp.exp(ml - m_new)
        c_new = s_old[..., None, None] * c_st + s_loc[..., None, None] * cl
        n_new = s_old[..., None] * n_st + s_loc[..., None] * nl
        return (c_new, n_new, m_new), (c_st, n_st, m_st)

    init = (jnp.zeros((bsz, ML_HEADS, ML_HEAD_DIM, ML_HEAD_DIM), jnp.float32),
            jnp.zeros((bsz, ML_HEADS, ML_HEAD_DIM), jnp.float32),
            jnp.zeros((bsz, ML_HEADS), jnp.float32))
    _, (c_prev, n_prev, m_prev) = lax.scan(
        step, init, (jnp.moveaxis(f_tot, 2, 0), jnp.moveaxis(m_loc, 2, 0),
                     jnp.moveaxis(c_loc, 2, 0), jnp.moveaxis(n_loc, 2, 0)))
    m_prev = jnp.moveaxis(m_prev, 0, 2)
    log_d = f_cs[..., :, None] - f_cs[..., None, :] + log_i[..., None, :]
    log_d = jnp.where(causal_mask(CHUNK), log_d, -jnp.inf)
    log_a = f_cs + m_prev[..., None]
    m_row = jnp.maximum(log_a, jnp.max(log_d, axis=-1))
    scores = jnp.einsum('bhcld,bhcsd->bhcls', qh, kh) * jnp.exp(log_d - m_row[..., None])
    scale_prev = jnp.exp(log_a - m_row)
    num = jnp.einsum('bhcls,bhcsd->bhcld', scores, vh) + scale_prev[..., None] * jnp.einsum('cbhde,bhcle->bhcld', c_prev, qh)
    den = jnp.sum(scores, axis=-1) + scale_prev * jnp.einsum('cbhe,bhcle->bhcl', n_prev, qh)
    h = num / jnp.maximum(jnp.abs(den), jnp.exp(-m_row))[..., None]
    h = h.transpose(0, 2, 3, 1, 4).reshape(bsz, s, BR_WIDTH)
    h = jax.nn.sigmoid(o_raw.astype(jnp.float32)) * h
    return headwise_rmsnorm(h, norm_g, ML_HEADS) * jax.nn.silu(z.astype(jnp.float32))


def retention_branch(q, k, v, z, norm_g):
    bsz, s, _ = q.shape
    pos = jnp.arange(s)
    qr = rope(q.reshape(bsz, s, RET_HEADS, RET_QK_DIM).astype(jnp.float32), pos)
    kr = rope(k.reshape(bsz, s, RET_HEADS, RET_QK_DIM).astype(jnp.float32), pos) * (RET_QK_DIM ** -0.5)
    qh = chunk_heads(qr.reshape(bsz, s, -1), RET_HEADS)
    kh = chunk_heads(kr.reshape(bsz, s, -1), RET_HEADS)
    vh = chunk_heads(v, RET_HEADS)
    log_g = jnp.log1p(-jnp.exp2(-5.0 - jnp.arange(RET_HEADS, dtype=jnp.float32)))
    idx = jnp.arange(CHUNK, dtype=jnp.float32)
    rel = idx[:, None] - idx[None, :]
    dmat = jnp.where(rel >= 0, jnp.exp(log_g[:, None, None] * jnp.maximum(rel, 0.0)), 0.0)
    inner = jnp.einsum('bhcld,bhcsd->bhcls', qh, kh) * dmat[None, :, None]
    y_in = jnp.einsum('bhcls,bhcse->bhcle', inner, vh)
    dec_end = jnp.exp(log_g[:, None] * (CHUNK - 1.0 - idx))
    r_loc = jnp.einsum('bhcld,hl,bhcle->bhcde', kh, dec_end, vh)
    chunk_dec = jnp.exp(log_g * CHUNK)

    def step(r_st, r_c):
        return chunk_dec[None, :, None, None] * r_st + r_c, r_st

    r0 = jnp.zeros((bsz, RET_HEADS, RET_QK_DIM, RET_V_DIM), jnp.float32)
    _, r_prev = lax.scan(step, r0, jnp.moveaxis(r_loc, 2, 0))
    dec_start = jnp.exp(log_g[:, None] * (idx + 1.0))
    y = y_in + jnp.einsum('bhcld,cbhde,hl->bhcle', qh, r_prev, dec_start)
    y = y.transpose(0, 2, 3, 1, 4).reshape(bsz, s, BR_WIDTH)
    return headwise_rmsnorm(y, norm_g, RET_HEADS) * jax.nn.silu(z.astype(jnp.float32))


def hybrid_layer(x, norm_g, w_in, a_conv_w, a_conv_b, a_gate_a_w, a_gate_a_b, a_gate_x_w,
                 a_gate_x_b, a_lambda, b_conv_w, b_conv_b, b_dt_bias, b_a_log, b_d_skip,
                 b_norm_g, c_i_bias, c_f_bias, c_norm_g, d_norm_g, w_branch, w_out):
    bsz, s, _ = x.shape
    hn = rmsnorm(x, norm_g)
    proj = jnp.einsum('bsd,dw->bsw', hn, w_in)
    (a_x, a_z, b_xbc, b_z, b_dt, c_q, c_k, c_v, c_o, c_z, c_i, c_f,
     d_q, d_k, d_v, d_z, gates) = jnp.split(proj, SPLIT_POINTS, axis=-1)
    y_a = rglru_branch(a_x, a_z, a_conv_w, a_conv_b, a_gate_a_w, a_gate_a_b, a_gate_x_w, a_gate_x_b, a_lambda)
    y_b = ssd_branch(b_xbc, b_z, b_dt, b_conv_w, b_conv_b, b_dt_bias, b_a_log, b_d_skip, b_norm_g)
    y_c = mlstm_branch(c_q, c_k, c_v, c_o, c_z, c_i, c_f, c_i_bias, c_f_bias, c_norm_g)
    y_d = retention_branch(d_q, d_k, d_v, d_z, d_norm_g)
    branches = jnp.stack([y_a, y_b, y_c, y_d], axis=2).astype(x.dtype)
    up = jnp.einsum('bsnw,nwd->bsnd', branches, w_branch)
    g = jax.nn.sigmoid(gates.reshape(bsz, s, N_BRANCH, D_MODEL))
    merged = jnp.sum(g * up, axis=2)
    return x + jnp.einsum('bsd,de->bse', merged, w_out)


def setup_inputs(seed: int = 0) -> dict:
    key = jax.random.key(seed)
    ks = jax.random.split(key, 24)
    f32 = jnp.float32

    def nrm(k, shape, scale):
        return jax.random.normal(k, shape, f32) * scale

    L = DEPTH
    u_lam = jax.random.uniform(ks[9], (L, BR_WIDTH), f32, 0.9, 0.999)
    base = u_lam ** (1.0 / RG_C)
    dt0 = jnp.exp(jax.random.uniform(ks[12], (L, SSD_HEADS), f32, np.log(0.001), np.log(0.1)))
    return {
        'x': nrm(ks[0], (BATCH, SEQ, D_MODEL), 1.0),
        'norm_g': 1.0 + nrm(ks[1], (L, D_MODEL), 0.02),
        'w_in': nrm(ks[2], (L, D_MODEL, W_IN), D_MODEL ** -0.5),
        'a_conv_w': nrm(ks[3], (L, CONV_W, BR_WIDTH), CONV_W ** -0.5),
        'a_conv_b': nrm(ks[4], (L, BR_WIDTH), 0.02),
        'a_gate_a_w': nrm(ks[5], (L, RG_BLOCKS, RG_BLOCK, RG_BLOCK), RG_BLOCK ** -0.5),
        'a_gate_a_b': nrm(ks[6], (L, BR_WIDTH), 0.02),
        'a_gate_x_w': nrm(ks[7], (L, RG_BLOCKS, RG_BLOCK, RG_BLOCK), RG_BLOCK ** -0.5),
        'a_gate_x_b': nrm(ks[8], (L, BR_WIDTH), 0.02),
        'a_lambda': jnp.log(base) - jnp.log1p(-base),
        'b_conv_w': nrm(ks[10], (L, CONV_W, SSD_XBC), CONV_W ** -0.5),
        'b_conv_b': nrm(ks[11], (L, SSD_XBC), 0.02),
        'b_dt_bias': dt0 + jnp.log(-jnp.expm1(-dt0)),
        'b_a_log': jnp.log(jax.random.uniform(ks[13], (L, SSD_HEADS), f32, 1.0, 16.0)),
        'b_d_skip': 1.0 + nrm(ks[14], (L, SSD_HEADS), 0.1),
        'b_norm_g': 1.0 + nrm(ks[15], (L, BR_WIDTH), 0.02),
        'c_i_bias': nrm(ks[16], (L, ML_HEADS), 0.1),
        'c_f_bias': 3.0 + nrm(ks[17], (L, ML_HEADS), 0.5),
        'c_norm_g': 1.0 + nrm(ks[18], (L, BR_WIDTH), 0.02),
        'd_norm_g': 1.0 + nrm(ks[19], (L, BR_WIDTH), 0.02),
        'w_branch': nrm(ks[20], (L, N_BRANCH, BR_WIDTH, D_MODEL), BR_WIDTH ** -0.5),
        'w_out': nrm(ks[21], (L, D_MODEL, D_MODEL), D_MODEL ** -0.5),
        'final_norm_g': 1.0 + nrm(ks[22], (D_MODEL,), 0.02),
    }


def reference(x, norm_g, w_in, a_conv_w, a_conv_b, a_gate_a_w, a_gate_a_b, a_gate_x_w,
              a_gate_x_b, a_lambda, b_conv_w, b_conv_b, b_dt_bias, b_a_log, b_d_skip,
              b_norm_g, c_i_bias, c_f_bias, c_norm_g, d_norm_g, w_branch, w_out, final_norm_g):
    for l in range(DEPTH):
        x = hybrid_layer(x, norm_g[l], w_in[l], a_conv_w[l], a_conv_b[l], a_gate_a_w[l],
                         a_gate_a_b[l], a_gate_x_w[l], a_gate_x_b[l], a_lambda[l],
                         b_conv_w[l], b_conv_b[l], b_dt_bias[l], b_a_log[l], b_d_skip[l],
                         b_norm_g[l], c_i_bias[l], c_f_bias[l], c_norm_g[l], d_norm_g[l],
                         w_branch[l], w_out[l])
    return rmsnorm(x, final_norm_g)
```

```python
import functools

import numpy as np
import jax
import jax.numpy as jnp
from jax import lax
from jax.experimental import pallas as pl
from jax.experimental.pallas import tpu as pltpu

F32 = jnp.float32
MXU_DTYPE = jnp.bfloat16

D_MODEL = 1024
BR = 512
EPS = 1e-6
CONV_W = 4
RG_BLOCKS = 8
RG_C = 8.0
SSD_HEADS = 8
SSD_GROUPS = 2
SSD_HPG = 4
SSD_HEAD_DIM = 64
SSD_STATE = 128
ML_HEADS = 4
ML_HEAD_DIM = 128
RET_HEADS = 4
RET_QK = 64
RET_V = 128
ROPE_BASE = 10000.0
LANES = 128
SUBLANES = 8

SEQ_TILE = 256
CHUNK = 128
VMEM_LIMIT_BYTES = 60 * 1024 * 1024

(R_NORM_G, R_FINAL_G, R_A_CONVW, R_A_CONVB, R_A_GATEB, R_A_LAM, R_B_CONVW, R_B_CONVB, R_B_DSKIP,
 R_B_NORMG, R_C_NORMG, R_D_NORMG, R_B_DTBIAS, R_B_ALOG, R_C_IBIAS, R_C_FBIAS) = (
    0, 1, 2, 6, 7, 8, 9, 13, 14, 15, 16, 17, 18, 19, 20, 21)
VEC_ROWS = 24

SMALL_DT, SMALL_I, SMALL_F = 0, 8, 12


def _mm(a, b):
    return jnp.dot(a.astype(MXU_DTYPE), b.astype(MXU_DTYPE), preferred_element_type=F32)


def _mm_nt(a, b):
    return lax.dot_general(a.astype(MXU_DTYPE), b.astype(MXU_DTYPE), (((1,), (1,)), ((), ())),
                           preferred_element_type=F32)


def _mm_tn(a, b):
    return lax.dot_general(a.astype(MXU_DTYPE), b.astype(MXU_DTYPE), (((0,), (0,)), ((), ())),
                           preferred_element_type=F32)


def _sigmoid(x):
    return 1.0 / (1.0 + jnp.exp(-x))


def _silu(x):
    return x * _sigmoid(x)


def _softplus(x):
    return jnp.maximum(x, 0.0) + jnp.log1p(jnp.exp(-jnp.abs(x)))


def _rmsnorm(x, g):
    return x * lax.rsqrt(jnp.mean(x * x, axis=-1, keepdims=True) + EPS) * g


def _headwise_rmsnorm(y, g, n_heads):
    w = y.shape[1] // n_heads
    parts = []
    for h in range(n_heads):
        yh = y[:, h * w:(h + 1) * w]
        parts.append(yh * lax.rsqrt(jnp.mean(yh * yh, axis=-1, keepdims=True) + EPS))
    return jnp.concatenate(parts, axis=1) * g


def _row_in_chunk(shape, period):
    return lax.broadcasted_iota(jnp.int32, shape, 0) & (period - 1)


def _cumsum_rows(v, rmod, period):
    k = 1
    while k < period:
        v = v + jnp.where(rmod >= k, pltpu.roll(v, k, 0), 0.0)
        k *= 2
    return v


def _scan_rows(a, b, rmod, period):
    k = 1
    while k < period:
        keep = rmod >= k
        b = a * jnp.where(keep, pltpu.roll(b, k, 0), 0.0) + b
        a = a * jnp.where(keep, pltpu.roll(a, k, 0), 1.0)
        k *= 2
    return a, b


def _chunk_last_bcast(v, lc):
    n = v.shape[0] // lc
    return jnp.concatenate(
        [jnp.broadcast_to(v[(c + 1) * lc - 1:(c + 1) * lc, :], (lc, v.shape[1])) for c in range(n)], axis=0)


def _head_bcast(v, n_heads, head_w):
    r = v.shape[0]
    cols = [jnp.broadcast_to(v[:, h:h + 1], (r, LANES)) for h in range(n_heads)]
    if head_w == LANES:
        return jnp.concatenate(cols, axis=1)
    assert head_w * 2 == LANES
    lane = lax.broadcasted_iota(jnp.int32, (r, LANES), 1)
    return jnp.concatenate(
        [jnp.where(lane < head_w, cols[2 * j], cols[2 * j + 1]) for j in range(n_heads // 2)], axis=1)


def _causal_conv(ext_ref, x, w, b, ts):
    ext_ref[SUBLANES:SUBLANES + ts, :] = x
    out = b + w[CONV_W - 1:CONV_W, :] * x
    for j in range(CONV_W - 1):
        off = SUBLANES - (CONV_W - 1) + j
        out = out + w[j:j + 1, :] * ext_ref[off:off + ts, :]
    ext_ref[0:SUBLANES, :] = x[ts - SUBLANES:ts, :]
    return out


def _layer_kernel(x_ref, vec_ref, wa_ref, wb_ref, wc_ref, wd_ref, ws_ref, wg_ref, agate_ref,
                  wbr_ref, wout_ref, rope_ref, dtab_ref, dmat_ref, o_ref,
                  a_ext, a_h, b_ext, b_state, c_state, c_n, c_m, d_state,
                  *, ts, lc, apply_final_norm):
    nchunk = ts // lc

    @pl.when(pl.program_id(1) == 0)
    def _():
        a_ext[0:SUBLANES, :] = jnp.zeros((SUBLANES, BR), F32)
        a_h[...] = jnp.zeros_like(a_h)
        b_ext[0:SUBLANES, :] = jnp.zeros((SUBLANES, 2 * BR), F32)
        b_state[...] = jnp.zeros_like(b_state)
        c_state[...] = jnp.zeros_like(c_state)
        c_n[...] = jnp.zeros_like(c_n)
        c_m[...] = jnp.zeros_like(c_m)
        d_state[...] = jnp.zeros_like(d_state)

    def vec(row, width, nrows=1):
        return vec_ref[row:row + nrows, 0:width]

    x = x_ref[0]
    hn = _rmsnorm(x, vec(R_NORM_G, D_MODEL)).astype(MXU_DTYPE)
    small = _mm(hn, ws_ref[...])

    rmod_small = _row_in_chunk((ts, LANES), lc)
    ri = lax.broadcasted_iota(jnp.int32, (lc, lc), 0)
    ci = lax.broadcasted_iota(jnp.int32, (lc, lc), 1)
    causal = ri >= ci
    neg_inf = jnp.float32(-jnp.inf)

    pa = _mm(hn, wa_ref[...])
    u, za = pa[:, 0:BR], pa[:, BR:2 * BR]
    xc = _causal_conv(a_ext, u, vec(R_A_CONVW, BR, CONV_W), vec(R_A_CONVB, BR), ts)
    gates = _mm(xc, agate_ref[...]) + vec(R_A_GATEB, 2 * BR)
    r_gate = _sigmoid(gates[:, 0:BR])
    i_gate = _sigmoid(gates[:, BR:2 * BR])
    log_a = (-RG_C) * r_gate * _softplus(-vec(R_A_LAM, BR))
    a = jnp.exp(log_a)
    drive = jnp.sqrt(1.0 - a * a) * (i_gate * xc)
    a_cum, h_loc = _scan_rows(a, drive, _row_in_chunk((ts, BR), ts), ts)
    h = h_loc + a_cum * a_h[0:1, :]
    a_h[0:1, :] = h[ts - 1:ts, :]
    y_a = h * _silu(za)

    pb = _mm(hn, wb_ref[...])
    zb = pb[:, 2 * BR:3 * BR]
    xbc = _silu(_causal_conv(b_ext, pb[:, 0:2 * BR], vec(R_B_CONVW, 2 * BR, CONV_W),
                             vec(R_B_CONVB, 2 * BR), ts))
    xh = xbc[:, 0:BR]
    bm = xbc[:, BR:BR + 2 * SSD_STATE]
    cm = xbc[:, BR + 2 * SSD_STATE:BR + 4 * SSD_STATE]
    dt = _softplus(small + vec(R_B_DTBIAS, LANES))
    a_dt = -jnp.exp(vec(R_B_ALOG, LANES)) * dt
    a_cs = _cumsum_rows(a_dt, rmod_small, lc)
    a_tot = _chunk_last_bcast(a_cs, lc)
    dt_f = _head_bcast(dt, SSD_HEADS, SSD_HEAD_DIM)
    eacs_f = _head_bcast(jnp.exp(a_cs), SSD_HEADS, SSD_HEAD_DIM)
    dte_f = _head_bcast(jnp.exp(a_tot - a_cs), SSD_HEADS, SSD_HEAD_DIM)
    x_dt = xh * dt_f
    xs = x_dt * dte_f
    gw = SSD_HPG * SSD_HEAD_DIM
    lane_head = lax.broadcasted_iota(jnp.int32, (lc, gw), 1) // SSD_HEAD_DIM
    y_rows = []
    for c in range(nchunk):
        r0, r1 = c * lc, (c + 1) * lc
        acs_c = a_cs[r0:r1, :]
        acs_t = acs_c.T
        y_groups = []
        for g in range(SSD_GROUPS):
            cg = cm[r0:r1, g * SSD_STATE:(g + 1) * SSD_STATE]
            bg = bm[r0:r1, g * SSD_STATE:(g + 1) * SSD_STATE]
            cb = _mm_nt(cg, bg)
            masked = []
            for e in range(SSD_HPG):
                hh = g * SSD_HPG + e
                diff = acs_c[:, hh:hh + 1] - acs_t[hh:hh + 1, :]
                decay = jnp.exp(jnp.where(causal, diff, neg_inf))
                masked.append((cb * decay).astype(MXU_DTYPE))
            xg = x_dt[r0:r1, g * gw:(g + 1) * gw]
            x_blockdiag = jnp.concatenate(
                [jnp.where(lane_head == e, xg, 0.0) for e in range(SSD_HPG)], axis=0)
            y_diag = _mm(jnp.concatenate(masked, axis=1), x_blockdiag)
            state = b_state[g]
            y_off = _mm(cg, state) * eacs_f[r0:r1, g * gw:(g + 1) * gw]
            y_groups.append(y_diag + y_off)
            b_state[g] = (eacs_f[r1 - 1:r1, g * gw:(g + 1) * gw] * state
                          + _mm_tn(bg, xs[r0:r1, g * gw:(g + 1) * gw]))
        y_rows.append(jnp.concatenate(y_groups, axis=1))
    y_b = jnp.concatenate(y_rows, axis=0) + vec(R_B_DSKIP, BR) * xh
    y_b = _rmsnorm(y_b * _silu(zb), vec(R_B_NORMG, BR))

    pc = _mm(hn, wc_ref[...])
    qc = pc[:, 0:BR]
    kc = pc[:, BR:2 * BR] * (ML_HEAD_DIM ** -0.5)
    vc = pc[:, 2 * BR:3 * BR]
    oc = pc[:, 3 * BR:4 * BR]
    zc = pc[:, 4 * BR:5 * BR]
    log_i = pltpu.roll(small, LANES - SMALL_I, 1) + vec(R_C_IBIAS, LANES)
    log_f = -_softplus(-(pltpu.roll(small, LANES - SMALL_F, 1) + vec(R_C_FBIAS, LANES)))
    f_cs = _cumsum_rows(log_f, rmod_small, lc)
    f_tot = _chunk_last_bcast(f_cs, lc)
    w_end = f_tot - f_cs + log_i
    g_src = log_i - f_cs
    m_state = c_m[0:1, :]
    h_rows = []
    for c in range(nchunk):
        r0, r1 = c * lc, (c + 1) * lc
        fcs_c = f_cs[r0:r1, :]
        w_c = w_end[r0:r1, :]
        m_loc = jnp.max(w_c, axis=0, keepdims=True)
        p_end = jnp.exp(w_c - m_loc)
        g_t = g_src[r0:r1, :].T
        ft = f_tot[r1 - 1:r1, :]
        m_new = jnp.maximum(ft + m_state, m_loc)
        s_old = jnp.exp(ft + m_state - m_new)
        s_loc = jnp.exp(m_loc - m_new)
        h_heads = []
        for hh in range(ML_HEADS):
            c0, c1 = hh * ML_HEAD_DIM, (hh + 1) * ML_HEAD_DIM
            qh, kh, vh = qc[r0:r1, c0:c1], kc[r0:r1, c0:c1], vc[r0:r1, c0:c1]
            fcol = fcs_c[:, hh:hh + 1]
            log_d = jnp.where(causal, fcol + g_t[hh:hh + 1, :], neg_inf)
            log_prev = fcol + m_state[:, hh:hh + 1]
            m_row = jnp.maximum(log_prev, jnp.max(log_d, axis=1, keepdims=True))
            scores = _mm_nt(qh, kh) * jnp.exp(log_d - m_row)
            scale_prev = jnp.exp(log_prev - m_row)
            c_prev = c_state[hh]
            n_prev = c_n[hh:hh + 1, :]
            num = _mm(scores, vh) + scale_prev * _mm_nt(qh, c_prev)
            den = (jnp.sum(scores, axis=1, keepdims=True)
                   + scale_prev * jnp.sum(qh * n_prev, axis=1, keepdims=True))
            inv = 1.0 / jnp.maximum(jnp.abs(den), jnp.exp(-m_row))
            h_heads.append(num * inv)
            p = p_end[:, hh:hh + 1]
            so, sl = s_old[:, hh:hh + 1], s_loc[:, hh:hh + 1]
            c_state[hh] = so * c_prev + sl * _mm_tn(p * vh, kh)
            c_n[hh:hh + 1, :] = so * n_prev + sl * jnp.sum(p * kh, axis=0, keepdims=True)
        m_state = m_new
        h_rows.append(jnp.concatenate(h_heads, axis=1))
    c_m[0:1, :] = m_state
    y_c = _sigmoid(oc) * jnp.concatenate(h_rows, axis=0)
    y_c = _headwise_rmsnorm(y_c, vec(R_C_NORMG, BR), ML_HEADS) * _silu(zc)

    pd = _mm(hn, wd_ref[...])
    qk_w = RET_HEADS * RET_QK
    vd = pd[:, 2 * qk_w:2 * qk_w + BR]
    zd = pd[:, 2 * qk_w + BR:2 * qk_w + 2 * BR]
    cos2 = jnp.concatenate([rope_ref[:, 0:LANES]] * 2, axis=1)
    sin2 = jnp.concatenate([rope_ref[:, LANES:2 * LANES]] * 2, axis=1)
    lane_q = lax.broadcasted_iota(jnp.int32, (ts, qk_w), 1)
    first_half = (lane_q & (RET_QK - 1)) < RET_QK // 2

    def rope(t):
        partner = jnp.where(first_half, pltpu.roll(t, qk_w - RET_QK // 2, 1), pltpu.roll(t, RET_QK // 2, 1))
        return t * cos2 + partner * sin2

    qr = rope(pd[:, 0:qk_w])
    kr = rope(pd[:, qk_w:2 * qk_w]) * (RET_QK ** -0.5)
    dec_start = dtab_ref[0:lc, 0:qk_w]
    dec_end = dtab_ref[0:lc, qk_w:2 * qk_w]
    lane_pair = lax.broadcasted_iota(jnp.int32, (lc, LANES), 1) // RET_QK
    y_rows = []
    for c in range(nchunk):
        r0, r1 = c * lc, (c + 1) * lc
        qr_c, kr_c = qr[r0:r1, :], kr[r0:r1, :]
        qd_c = qr_c * dec_start
        kd_c = kr_c * dec_end
        y_heads = []
        for hh in range(RET_HEADS):
            p0, p1 = (hh // 2) * LANES, (hh // 2 + 1) * LANES
            mine = lane_pair == (hh % 2)
            vh = vd[r0:r1, hh * RET_V:(hh + 1) * RET_V]
            inner = _mm_nt(jnp.where(mine, qr_c[:, p0:p1], 0.0), kr_c[:, p0:p1]) * dmat_ref[hh]
            r_prev = d_state[hh]
            y_heads.append(_mm(inner, vh) + _mm(qd_c[:, p0:p1], r_prev))
            chunk_dec = dtab_ref[lc + hh:lc + hh + 1, 0:RET_V]
            d_state[hh] = chunk_dec * r_prev + _mm_tn(jnp.where(mine, kd_c[:, p0:p1], 0.0), vh)
        y_rows.append(jnp.concatenate(y_heads, axis=1))
    y_d = _headwise_rmsnorm(jnp.concatenate(y_rows, axis=0), vec(R_D_NORMG, BR), RET_HEADS) * _silu(zd)

    merged = jnp.zeros((ts, D_MODEL), F32)
    for n, y in enumerate((y_a, y_b, y_c, y_d)):
        gate = _sigmoid(_mm(hn, wg_ref[:, n * D_MODEL:(n + 1) * D_MODEL]))
        merged = merged + gate * _mm(y, wbr_ref[n])
    out = x + _mm(merged, wout_ref[...])
    if apply_final_norm:
        out = _rmsnorm(out, vec(R_FINAL_G, D_MODEL))
    o_ref[0] = out


def _const_spec(shape):
    zeros = (0,) * len(shape)
    return pl.BlockSpec(shape, lambda b, s: zeros, pipeline_mode=pl.Buffered(1))


def _layer_call(x, vecs, wa, wb, wc, wd, ws, wg, agate, wbr, wout, rope_tab, dtab, dmat,
                *, ts, lc, apply_final_norm):
    bsz, seq, d = x.shape
    assert d == D_MODEL and seq % ts == 0 and ts % lc == 0 and lc % LANES == 0
    kern = functools.partial(_layer_kernel, ts=ts, lc=lc, apply_final_norm=apply_final_norm)
    consts = (vecs, wa, wb, wc, wd, ws, wg, agate, wbr, wout)
    in_specs = ([pl.BlockSpec((1, ts, d), lambda b, s: (b, s, 0))]
                + [_const_spec(c.shape) for c in consts]
                + [pl.BlockSpec((ts, 2 * LANES), lambda b, s: (s, 0)),
                   _const_spec(dtab.shape), _const_spec(dmat.shape)])
    scratch = [
        pltpu.VMEM((ts + SUBLANES, BR), F32),
        pltpu.VMEM((SUBLANES, BR), F32),
        pltpu.VMEM((ts + SUBLANES, 2 * BR), F32),
        pltpu.VMEM((SSD_GROUPS, SSD_STATE, SSD_HPG * SSD_HEAD_DIM), F32),
        pltpu.VMEM((ML_HEADS, ML_HEAD_DIM, ML_HEAD_DIM), F32),
        pltpu.VMEM((SUBLANES, LANES), F32),
        pltpu.VMEM((SUBLANES, LANES), F32),
        pltpu.VMEM((RET_HEADS, LANES, RET_V), F32),
    ]
    return pl.pallas_call(
        kern,
        grid=(bsz, seq // ts),
        in_specs=in_specs,
        out_specs=pl.BlockSpec((1, ts, d), lambda b, s: (b, s, 0)),
        out_shape=jax.ShapeDtypeStruct(x.shape, x.dtype),
        scratch_shapes=scratch,
        compiler_params=pltpu.CompilerParams(
            dimension_semantics=("arbitrary", "arbitrary"),
            vmem_limit_bytes=VMEM_LIMIT_BYTES),
        name="hybrid_layer",
    )(x, *consts, rope_tab, dtab, dmat)


def _pad_lanes(v, width):
    return jnp.pad(v, (0, width - v.shape[0]))


def _retention_tables(seq, lc):
    half = RET_QK // 2
    inv = ROPE_BASE ** (-jnp.arange(half, dtype=F32) / half)
    ang = jnp.arange(seq).astype(F32)[:, None] * inv
    cos, sin = jnp.cos(ang), jnp.sin(ang)
    cos128 = jnp.tile(cos, (1, LANES // half))
    sin128 = jnp.tile(jnp.concatenate([-sin, sin], axis=1), (1, LANES // RET_QK))
    rope_tab = jnp.concatenate([cos128, sin128], axis=1)
    log_g = jnp.log1p(-jnp.exp2(-5.0 - jnp.arange(RET_HEADS, dtype=F32)))
    idx = jnp.arange(lc, dtype=F32)
    rel = idx[:, None] - idx[None, :]
    dmat = jnp.where(rel >= 0, jnp.exp(log_g[:, None, None] * jnp.maximum(rel, 0.0)), 0.0)
    dec_start = jnp.repeat(jnp.exp(log_g[None, :] * (idx[:, None] + 1.0)), RET_QK, axis=1)
    dec_end = jnp.repeat(jnp.exp(log_g[None, :] * (lc - 1.0 - idx[:, None])), RET_QK, axis=1)
    chunk_dec = jnp.broadcast_to(jnp.exp(log_g * lc)[:, None], (RET_HEADS, 2 * RET_HEADS * RET_QK))
    dtab = jnp.concatenate([jnp.concatenate([dec_start, dec_end], axis=1), chunk_dec,
                            jnp.zeros((SUBLANES - RET_HEADS, 2 * RET_HEADS * RET_QK), F32)], axis=0)
    return rope_tab, dtab, dmat


def _layer_operands(l, norm_g, w_in, a_conv_w, a_conv_b, a_gate_a_w, a_gate_a_b, a_gate_x_w, a_gate_x_b,
                    a_lambda, b_conv_w, b_conv_b, b_dt_bias, b_a_log, b_d_skip, b_norm_g, c_i_bias,
                    c_f_bias, c_norm_g, d_norm_g, w_branch, w_out, final_norm_g):
    w = w_in[l]
    o = 0
    a0 = o; o += 2 * BR
    b0 = o; o += 3 * BR
    dt0 = o; o += SSD_HEADS
    c0 = o; o += 5 * BR
    if0 = o; o += 2 * ML_HEADS
    d0 = o; o += 2 * RET_HEADS * RET_QK + 2 * BR
    g0 = o; o += 4 * D_MODEL
    assert o == w.shape[1]
    bf = lambda t: t.astype(MXU_DTYPE)
    wa, wb, wc, wd, wg = (bf(w[:, a0:b0]), bf(w[:, b0:dt0]), bf(w[:, c0:if0]), bf(w[:, d0:g0]), bf(w[:, g0:o]))
    ws = bf(jnp.concatenate([w[:, dt0:c0], w[:, if0:d0],
                             jnp.zeros((D_MODEL, LANES - SSD_HEADS - 2 * ML_HEADS), w.dtype)], axis=1))
    eye = jnp.eye(RG_BLOCKS, dtype=F32)
    blockdiag = lambda t: (eye[:, None, :, None] * t[:, :, None, :]).reshape(BR, BR)
    agate = bf(jnp.concatenate([blockdiag(a_gate_a_w[l]), blockdiag(a_gate_x_w[l])], axis=1))
    rows = [None] * VEC_ROWS
    rows[R_NORM_G] = norm_g[l]
    rows[R_FINAL_G] = final_norm_g
    for j in range(CONV_W):
        rows[R_A_CONVW + j] = a_conv_w[l, j]
        rows[R_B_CONVW + j] = b_conv_w[l, j]
    rows[R_A_CONVB] = a_conv_b[l]
    rows[R_A_GATEB] = jnp.concatenate([a_gate_a_b[l], a_gate_x_b[l]])
    rows[R_A_LAM] = a_lambda[l]
    rows[R_B_CONVB] = b_conv_b[l]
    rows[R_B_DSKIP] = jnp.repeat(b_d_skip[l], SSD_HEAD_DIM)
    rows[R_B_NORMG] = b_norm_g[l]
    rows[R_C_NORMG] = c_norm_g[l]
    rows[R_D_NORMG] = d_norm_g[l]
    rows[R_B_DTBIAS] = b_dt_bias[l]
    rows[R_B_ALOG] = b_a_log[l]
    rows[R_C_IBIAS] = c_i_bias[l]
    rows[R_C_FBIAS] = c_f_bias[l]
    vecs = jnp.stack([_pad_lanes(r.astype(F32), D_MODEL) if r is not None else jnp.zeros((D_MODEL,), F32)
                      for r in rows])
    return vecs, wa, wb, wc, wd, ws, wg, agate, bf(w_branch[l]), bf(w_out[l])


def kernel(x, norm_g, w_in, a_conv_w, a_conv_b, a_gate_a_w, a_gate_a_b, a_gate_x_w, a_gate_x_b, a_lambda,
           b_conv_w, b_conv_b, b_dt_bias, b_a_log, b_d_skip, b_norm_g, c_i_bias, c_f_bias, c_norm_g,
           d_norm_g, w_branch, w_out, final_norm_g):
    depth = w_in.shape[0]
    seq = x.shape[1]
    ts = min(SEQ_TILE, seq)
    lc = min(CHUNK, ts)
    rope_tab, dtab, dmat = _retention_tables(seq, lc)
    params = (norm_g, w_in, a_conv_w, a_conv_b, a_gate_a_w, a_gate_a_b, a_gate_x_w, a_gate_x_b, a_lambda,
              b_conv_w, b_conv_b, b_dt_bias, b_a_log, b_d_skip, b_norm_g, c_i_bias, c_f_bias, c_norm_g,
              d_norm_g, w_branch, w_out, final_norm_g)
    for l in range(depth):
        ops = _layer_operands(l, *params)
        x = _layer_call(x, *ops, rope_tab, dtab, dmat, ts=ts, lc=lc,
                        apply_final_norm=(l == depth - 1))
    return x
```

```python
import functools

import jax
import jax.numpy as jnp
from jax import lax
from jax.experimental import pallas as pl
from jax.experimental.pallas import tpu as pltpu

F32 = jnp.float32
MXU_DTYPE = jnp.bfloat16

D_MODEL = 1024
BR = 512
EPS = 1e-6
CONV_W = 4
RG_BLOCKS = 8
RG_C = 8.0
SSD_HEADS = 8
SSD_GROUPS = 2
SSD_HPG = 4
SSD_HEAD_DIM = 64
SSD_STATE = 128
ML_HEADS = 4
ML_HEAD_DIM = 128
RET_HEADS = 4
RET_QK = 64
RET_V = 128
ROPE_BASE = 10000.0
LANES = 128
SUBLANES = 8

SEQ_TILE = 512
CHUNK = 128
VMEM_LIMIT_BYTES = 60 * 1024 * 1024

(R_NORM_G, R_FINAL_G, R_A_CONVW, R_A_CONVB, R_A_GATEB, R_A_LAM, R_B_CONVW, R_B_CONVB, R_B_DSKIP,
 R_B_NORMG, R_C_NORMG, R_D_NORMG, R_B_DTBIAS, R_B_ALOG, R_C_IBIAS, R_C_FBIAS) = (
    0, 1, 2, 6, 7, 8, 9, 13, 14, 15, 16, 17, 18, 19, 20, 21)
VEC_ROWS = 24

SMALL_DT, SMALL_I, SMALL_F = 0, 8, 12


def _mm(a, b):
    return jnp.dot(a.astype(MXU_DTYPE), b.astype(MXU_DTYPE), preferred_element_type=F32)


def _mm_nt(a, b):
    return lax.dot_general(a.astype(MXU_DTYPE), b.astype(MXU_DTYPE), (((1,), (1,)), ((), ())),
                           preferred_element_type=F32)


def _mm_tn(a, b):
    return lax.dot_general(a.astype(MXU_DTYPE), b.astype(MXU_DTYPE), (((0,), (0,)), ((), ())),
                           preferred_element_type=F32)


def _sigmoid(x):
    return 0.5 * jnp.tanh(0.5 * x) + 0.5


def _silu(x):
    return x * _sigmoid(x)


def _softplus(x):
    return jnp.maximum(x, 0.0) + jnp.log1p(jnp.exp(-jnp.abs(x)))


def _rmsnorm(x, g):
    return x * lax.rsqrt(jnp.mean(x * x, axis=-1, keepdims=True) + EPS) * g


def _headwise_rmsnorm(y, g, n_heads):
    w = y.shape[1] // n_heads
    parts = []
    for h in range(n_heads):
        yh = y[:, h * w:(h + 1) * w]
        parts.append(yh * lax.rsqrt(jnp.mean(yh * yh, axis=-1, keepdims=True) + EPS))
    return jnp.concatenate(parts, axis=1) * g


def _to_strided(x, lc):
    nv = lc // SUBLANES
    return jnp.concatenate(
        [jnp.swapaxes(x[c * lc:(c + 1) * lc, :].reshape(SUBLANES, nv, x.shape[1]), 0, 1).reshape(lc, x.shape[1])
         for c in range(x.shape[0] // lc)], axis=0)


def _from_strided(x, lc):
    nv = lc // SUBLANES
    return jnp.concatenate(
        [jnp.swapaxes(x[c * lc:(c + 1) * lc, :].reshape(nv, SUBLANES, x.shape[1]), 0, 1).reshape(lc, x.shape[1])
         for c in range(x.shape[0] // lc)], axis=0)


def _strided_time(shape, axis, lc):
    i = lax.broadcasted_iota(jnp.int32, shape, axis)
    return (i & (SUBLANES - 1)) * (lc // SUBLANES) + (i >> 3)


def _sublane(width):
    return lax.broadcasted_iota(jnp.int32, (SUBLANES, width), 0)


def _slabs(v, r0, n):
    return [v[r0 + i * SUBLANES:r0 + (i + 1) * SUBLANES, :] for i in range(n)]


def _cumsum_time(v, lc):
    nv = lc // SUBLANES
    sub = _sublane(v.shape[1])
    out = []
    for c in range(v.shape[0] // lc):
        acc = _slabs(v, c * lc, nv)
        for i in range(1, nv):
            acc[i] = acc[i - 1] + acc[i]
        tot = acc[-1]
        inc = tot
        for k in (1, 2, 4):
            inc = inc + jnp.where(sub >= k, pltpu.roll(inc, k, 0), 0.0)
        out += [a + (inc - tot) for a in acc]
    return jnp.concatenate(out, axis=0)


def _scan_time(a, d, h_prev, lc):
    nv = lc // SUBLANES
    sub = _sublane(a.shape[1])
    out = []
    for c in range(a.shape[0] // lc):
        p, h = _slabs(a, c * lc, nv), _slabs(d, c * lc, nv)
        for i in range(1, nv):
            h[i] = p[i] * h[i - 1] + h[i]
            p[i] = p[i] * p[i - 1]
        pt, ht = p[-1], h[-1]
        for k in (1, 2, 4):
            keep = sub >= k
            ht = pt * jnp.where(keep, pltpu.roll(ht, k, 0), 0.0) + ht
            pt = pt * jnp.where(keep, pltpu.roll(pt, k, 0), 1.0)
        ends = ht + pt * h_prev
        carry = jnp.where(sub == 0, h_prev, pltpu.roll(ends, 1, 0))
        out += [h[i] + p[i] * carry for i in range(nv)]
        h_prev = ends[SUBLANES - 1:SUBLANES, :]
    return jnp.concatenate(out, axis=0), h_prev


def _chunk_last_bcast(v, lc):
    n = v.shape[0] // lc
    return jnp.concatenate(
        [jnp.broadcast_to(v[(c + 1) * lc - 1:(c + 1) * lc, :], (lc, v.shape[1])) for c in range(n)], axis=0)


def _head_bcast(v, n_heads, head_w):
    r = v.shape[0]
    cols = [jnp.broadcast_to(v[:, h:h + 1], (r, LANES)) for h in range(n_heads)]
    if head_w == LANES:
        return jnp.concatenate(cols, axis=1)
    assert head_w * 2 == LANES
    lane = lax.broadcasted_iota(jnp.int32, (r, LANES), 1)
    return jnp.concatenate(
        [jnp.where(lane < head_w, cols[2 * j], cols[2 * j + 1]) for j in range(n_heads // 2)], axis=1)


def _block_rows(x, n_blocks):
    rows, w = x.shape
    bl = w // n_blocks
    n_tiles = w // LANES
    zero = jnp.zeros((rows, LANES), MXU_DTYPE)
    half = lax.broadcasted_iota(jnp.int32, (rows, LANES), 1) // bl if bl < LANES else None
    out = []
    for h in range(n_blocks):
        tiles = []
        for j in range(n_tiles):
            if bl >= LANES:
                keep = (j * LANES) // bl == h
                tiles.append(x[:, j * LANES:(j + 1) * LANES].astype(MXU_DTYPE) if keep else zero)
            elif j == (h * bl) // LANES:
                tiles.append(jnp.where(half == h % (LANES // bl), x[:, j * LANES:(j + 1) * LANES], 0.0)
                             .astype(MXU_DTYPE))
            else:
                tiles.append(zero)
        out.append(jnp.concatenate(tiles, axis=1))
    return jnp.concatenate(out, axis=0)


def _block_diag(blocks):
    n = len(blocks)
    zero = jnp.zeros(blocks[0].shape, MXU_DTYPE)
    return jnp.concatenate(
        [jnp.concatenate([blocks[i].astype(MXU_DTYPE) if j == i else zero for j in range(n)], axis=1)
         for i in range(n)], axis=0)


def _lane_bcast_heads(cols, width):
    return jnp.concatenate([jnp.broadcast_to(c, (c.shape[0], width)) for c in cols], axis=1)


def _causal_conv(tail_ref, x, w, b, lc):
    nt = (CONV_W - 1) * SUBLANES
    sub0 = _sublane(x.shape[1]) == 0
    prev_tail = tail_ref[...]
    out = []
    for c in range(x.shape[0] // lc):
        xc = x[c * lc:(c + 1) * lc, :]
        cur_tail = xc[lc - nt:lc, :]
        wrap = jnp.concatenate(
            [jnp.where(sub0, pltpu.roll(p, 1, 0), pltpu.roll(q, 1, 0))
             for p, q in zip(_slabs(prev_tail, 0, CONV_W - 1), _slabs(cur_tail, 0, CONV_W - 1))], axis=0)
        acc = b + w[CONV_W - 1:CONV_W, :] * xc
        for sh in range(1, CONV_W):
            shifted = jnp.concatenate([wrap[nt - sh * SUBLANES:nt, :], xc[0:lc - sh * SUBLANES, :]], axis=0)
            acc = acc + w[CONV_W - 1 - sh:CONV_W - sh, :] * shifted
        out.append(acc)
        prev_tail = cur_tail
    tail_ref[...] = prev_tail
    return jnp.concatenate(out, axis=0)


def _layer_kernel(x_ref, vec_ref, wa_ref, wb_ref, wc_ref, wd_ref, ws_ref, wg_ref, agate_ref,
                  wbr_ref, wout_ref, rope_ref, dtab_ref, dmat_ref, o_ref,
                  a_tail, a_h, b_tail, b_state, c_state, c_n, c_m, d_state,
                  *, ts, lc, first_layer, last_layer):
    nchunk = ts // lc

    @pl.when(pl.program_id(1) == 0)
    def _():
        a_tail[...] = jnp.zeros_like(a_tail)
        a_h[...] = jnp.zeros_like(a_h)
        b_tail[...] = jnp.zeros_like(b_tail)
        b_state[...] = jnp.zeros_like(b_state)
        c_state[...] = jnp.zeros_like(c_state)
        c_n[...] = jnp.zeros_like(c_n)
        c_m[...] = jnp.zeros_like(c_m)
        d_state[...] = jnp.zeros_like(d_state)

    def vec(row, width, nrows=1):
        return vec_ref[row:row + nrows, 0:width]

    x = _to_strided(x_ref[0], lc) if first_layer else x_ref[0]
    hn = _rmsnorm(x, vec(R_NORM_G, D_MODEL)).astype(MXU_DTYPE)
    small = _mm(hn, ws_ref[...])

    def gated_up(n, y):
        gate = _sigmoid(_mm(hn, wg_ref[:, n * D_MODEL:(n + 1) * D_MODEL]))
        return gate * _mm(y, wbr_ref[n])

    causal = _strided_time((lc, lc), 0, lc) >= _strided_time((lc, lc), 1, lc)
    neg_inf = jnp.float32(-jnp.inf)

    pa = _mm(hn, wa_ref[...])
    u, za = pa[:, 0:BR], pa[:, BR:2 * BR]
    xc = _causal_conv(a_tail, u, vec(R_A_CONVW, BR, CONV_W), vec(R_A_CONVB, BR), lc)
    gates = _mm(xc, agate_ref[...]) + vec(R_A_GATEB, 2 * BR)
    r_gate = _sigmoid(gates[:, 0:BR])
    i_gate = _sigmoid(gates[:, BR:2 * BR])
    log_a = (-RG_C) * r_gate * _softplus(-vec(R_A_LAM, BR))
    a = jnp.exp(log_a)
    drive = jnp.sqrt(1.0 - a * a) * (i_gate * xc)
    h, h_last = _scan_time(a, drive, a_h[0:1, :], lc)
    a_h[0:1, :] = h_last
    merged = gated_up(0, h * _silu(za))

    pb = _mm(hn, wb_ref[...])
    zb = pb[:, 2 * BR:3 * BR]
    xbc = _silu(_causal_conv(b_tail, pb[:, 0:2 * BR], vec(R_B_CONVW, 2 * BR, CONV_W),
                             vec(R_B_CONVB, 2 * BR), lc))
    xh = xbc[:, 0:BR]
    bm = xbc[:, BR:BR + 2 * SSD_STATE]
    cm = xbc[:, BR + 2 * SSD_STATE:BR + 4 * SSD_STATE]
    dt = _softplus(small + vec(R_B_DTBIAS, LANES))
    a_dt = -jnp.exp(vec(R_B_ALOG, LANES)) * dt
    a_cs = _cumsum_time(a_dt, lc)
    a_tot = _chunk_last_bcast(a_cs, lc)
    dt_f = _head_bcast(dt, SSD_HEADS, SSD_HEAD_DIM)
    eacs_f = _head_bcast(jnp.exp(a_cs), SSD_HEADS, SSD_HEAD_DIM)
    dte_f = _head_bcast(jnp.exp(a_tot - a_cs), SSD_HEADS, SSD_HEAD_DIM)
    x_dt = xh * dt_f
    xs = x_dt * dte_f
    gw = SSD_HPG * SSD_HEAD_DIM
    y_rows = []
    for c in range(nchunk):
        r0, r1 = c * lc, (c + 1) * lc
        acs_c = a_cs[r0:r1, :]
        acs_t = acs_c.T
        c_c, b_c = cm[r0:r1, :], bm[r0:r1, :]
        cb = _mm_nt(c_c, _block_rows(b_c, SSD_GROUPS))
        y_groups = []
        for g in range(SSD_GROUPS):
            cb_g = cb[:, g * lc:(g + 1) * lc]
            masked = []
            for e in range(SSD_HPG):
                hh = g * SSD_HPG + e
                diff = acs_c[:, hh:hh + 1] - acs_t[hh:hh + 1, :]
                decay = jnp.exp(jnp.where(causal, diff, neg_inf))
                masked.append((cb_g * decay).astype(MXU_DTYPE))
            y_groups.append(_mm(jnp.concatenate(masked, axis=1),
                                _block_rows(x_dt[r0:r1, g * gw:(g + 1) * gw], SSD_HPG)))
        states = [b_state[g] for g in range(SSD_GROUPS)]
        y_off = _mm(c_c, _block_diag(states)) * eacs_f[r0:r1, :]
        y_rows.append(jnp.concatenate(y_groups, axis=1) + y_off)
        upd = _mm_tn(b_c, xs[r0:r1, :])
        for g in range(SSD_GROUPS):
            b_state[g] = (eacs_f[r1 - 1:r1, g * gw:(g + 1) * gw] * states[g]
                          + upd[g * SSD_STATE:(g + 1) * SSD_STATE, g * gw:(g + 1) * gw])
    y_b = jnp.concatenate(y_rows, axis=0) + vec(R_B_DSKIP, BR) * xh
    merged = merged + gated_up(1, _rmsnorm(y_b * _silu(zb), vec(R_B_NORMG, BR)))

    pc = _mm(hn, wc_ref[...])
    qc = pc[:, 0:BR]
    kc = pc[:, BR:2 * BR] * (ML_HEAD_DIM ** -0.5)
    vc = pc[:, 2 * BR:3 * BR]
    oc = pc[:, 3 * BR:4 * BR]
    zc = pc[:, 4 * BR:5 * BR]
    log_i = pltpu.roll(small, LANES - SMALL_I, 1) + vec(R_C_IBIAS, LANES)
    log_f = -_softplus(-(pltpu.roll(small, LANES - SMALL_F, 1) + vec(R_C_FBIAS, LANES)))
    f_cs = _cumsum_time(log_f, lc)
    f_tot = _chunk_last_bcast(f_cs, lc)
    w_end = f_tot - f_cs + log_i
    g_src = log_i - f_cs
    m_state = c_m[0:1, :]
    h_rows = []
    for c in range(nchunk):
        r0, r1 = c * lc, (c + 1) * lc
        fcs_c = f_cs[r0:r1, :]
        w_c = w_end[r0:r1, :]
        m_loc = jnp.max(w_c, axis=0, keepdims=True)
        p_end = jnp.exp(w_c - m_loc)
        g_t = g_src[r0:r1, :].T
        ft = f_tot[r1 - 1:r1, :]
        m_new = jnp.maximum(ft + m_state, m_loc)
        s_old = jnp.exp(ft + m_state - m_new)
        s_loc = jnp.exp(m_loc - m_new)
        q_c, k_c, v_c = qc[r0:r1, :], kc[r0:r1, :], vc[r0:r1, :]
        qk = _mm_nt(q_c, _block_rows(k_c, ML_HEADS))
        n_prev = c_n[0:1, :]
        qn = q_c * n_prev
        scores, scale_prev, inv = [], [], []
        for hh in range(ML_HEADS):
            fcol = fcs_c[:, hh:hh + 1]
            log_d = jnp.where(causal, fcol + g_t[hh:hh + 1, :], neg_inf)
            log_prev = fcol + m_state[:, hh:hh + 1]
            m_row = jnp.maximum(log_prev, jnp.max(log_d, axis=1, keepdims=True))
            s_h = qk[:, hh * lc:(hh + 1) * lc] * jnp.exp(log_d - m_row)
            sp = jnp.exp(log_prev - m_row)
            den = (jnp.sum(s_h, axis=1, keepdims=True)
                   + sp * jnp.sum(qn[:, hh * ML_HEAD_DIM:(hh + 1) * ML_HEAD_DIM], axis=1, keepdims=True))
            scores.append(s_h.astype(MXU_DTYPE))
            scale_prev.append(sp)
            inv.append(1.0 / jnp.maximum(jnp.abs(den), jnp.exp(-m_row)))
        c_prev = [c_state[hh] for hh in range(ML_HEADS)]
        num = (_mm(jnp.concatenate(scores, axis=1), _block_rows(v_c, ML_HEADS))
               + _lane_bcast_heads(scale_prev, ML_HEAD_DIM) * _mm_nt(q_c, _block_diag(c_prev)))
        h_rows.append(num * _lane_bcast_heads(inv, ML_HEAD_DIM))
        p_f = _lane_bcast_heads([p_end[:, hh:hh + 1] for hh in range(ML_HEADS)], ML_HEAD_DIM)
        c_loc = _mm_tn(p_f * v_c, k_c)
        for hh in range(ML_HEADS):
            d0, d1 = hh * ML_HEAD_DIM, (hh + 1) * ML_HEAD_DIM
            c_state[hh] = s_old[:, hh:hh + 1] * c_prev[hh] + s_loc[:, hh:hh + 1] * c_loc[d0:d1, d0:d1]
        so_f = _lane_bcast_heads([s_old[:, hh:hh + 1] for hh in range(ML_HEADS)], ML_HEAD_DIM)
        sl_f = _lane_bcast_heads([s_loc[:, hh:hh + 1] for hh in range(ML_HEADS)], ML_HEAD_DIM)
        c_n[0:1, :] = so_f * n_prev + sl_f * jnp.sum(p_f * k_c, axis=0, keepdims=True)
        m_state = m_new
    c_m[0:1, :] = m_state
    y_c = _sigmoid(oc) * jnp.concatenate(h_rows, axis=0)
    merged = merged + gated_up(2, _headwise_rmsnorm(y_c, vec(R_C_NORMG, BR), ML_HEADS) * _silu(zc))

    pd = _mm(hn, wd_ref[...])
    qk_w = RET_HEADS * RET_QK
    vd = pd[:, 2 * qk_w:2 * qk_w + BR]
    zd = pd[:, 2 * qk_w + BR:2 * qk_w + 2 * BR]
    cos2 = jnp.concatenate([rope_ref[:, 0:LANES]] * 2, axis=1)
    sin2 = jnp.concatenate([rope_ref[:, LANES:2 * LANES]] * 2, axis=1)
    lane_q = lax.broadcasted_iota(jnp.int32, (ts, qk_w), 1)
    first_half = (lane_q & (RET_QK - 1)) < RET_QK // 2

    def rope(t):
        partner = jnp.where(first_half, pltpu.roll(t, qk_w - RET_QK // 2, 1), pltpu.roll(t, RET_QK // 2, 1))
        return t * cos2 + partner * sin2

    qr = rope(pd[:, 0:qk_w])
    kr = rope(pd[:, qk_w:2 * qk_w]) * (RET_QK ** -0.5)
    dec_start = dtab_ref[0:lc, 0:qk_w]
    dec_end = dtab_ref[0:lc, qk_w:2 * qk_w]
    y_rows = []
    for c in range(nchunk):
        r0, r1 = c * lc, (c + 1) * lc
        qr_c, kr_c, v_c = qr[r0:r1, :], kr[r0:r1, :], vd[r0:r1, :]
        inner = _mm_nt(qr_c, _block_rows(kr_c, RET_HEADS)) * dmat_ref[...]
        r_prev = [d_state[hh] for hh in range(RET_HEADS)]
        y_rows.append(_mm(inner, _block_rows(v_c, RET_HEADS))
                      + _mm(qr_c * dec_start, _block_diag(r_prev)))
        upd = _mm_tn(kr_c * dec_end, v_c)
        for hh in range(RET_HEADS):
            chunk_dec = dtab_ref[lc + hh:lc + hh + 1, 0:RET_V]
            d_state[hh] = chunk_dec * r_prev[hh] + upd[hh * RET_QK:(hh + 1) * RET_QK, hh * RET_V:(hh + 1) * RET_V]
    y_d = _headwise_rmsnorm(jnp.concatenate(y_rows, axis=0), vec(R_D_NORMG, BR), RET_HEADS) * _silu(zd)

    merged = merged + gated_up(3, y_d)
    out = x + _mm(merged, wout_ref[...])
    if last_layer:
        out = _from_strided(_rmsnorm(out, vec(R_FINAL_G, D_MODEL)), lc)
    o_ref[0] = out


def _const_spec(shape):
    zeros = (0,) * len(shape)
    return pl.BlockSpec(shape, lambda b, s: zeros, pipeline_mode=pl.Buffered(1))


def _layer_call(x, vecs, wa, wb, wc, wd, ws, wg, agate, wbr, wout, rope_tab, dtab, dmat,
                *, ts, lc, first_layer, last_layer):
    bsz, seq, d = x.shape
    assert d == D_MODEL and seq % ts == 0 and ts % lc == 0 and lc % LANES == 0
    kern = functools.partial(_layer_kernel, ts=ts, lc=lc, first_layer=first_layer, last_layer=last_layer)
    consts = (vecs, wa, wb, wc, wd, ws, wg, agate, wbr, wout)
    in_specs = ([pl.BlockSpec((1, ts, d), lambda b, s: (b, s, 0))]
                + [_const_spec(c.shape) for c in consts]
                + [pl.BlockSpec((ts, 2 * LANES), lambda b, s: (s, 0)),
                   _const_spec(dtab.shape), _const_spec(dmat.shape)])
    scratch = [
        pltpu.VMEM(((CONV_W - 1) * SUBLANES, BR), F32),
        pltpu.VMEM((SUBLANES, BR), F32),
        pltpu.VMEM(((CONV_W - 1) * SUBLANES, 2 * BR), F32),
        pltpu.VMEM((SSD_GROUPS, SSD_STATE, SSD_HPG * SSD_HEAD_DIM), F32),
        pltpu.VMEM((ML_HEADS, ML_HEAD_DIM, ML_HEAD_DIM), F32),
        pltpu.VMEM((SUBLANES, BR), F32),
        pltpu.VMEM((SUBLANES, LANES), F32),
        pltpu.VMEM((RET_HEADS, RET_QK, RET_V), F32),
    ]
    return pl.pallas_call(
        kern,
        grid=(bsz, seq // ts),
        in_specs=in_specs,
        out_specs=pl.BlockSpec((1, ts, d), lambda b, s: (b, s, 0)),
        out_shape=jax.ShapeDtypeStruct(x.shape, x.dtype),
        scratch_shapes=scratch,
        compiler_params=pltpu.CompilerParams(
            dimension_semantics=("arbitrary", "arbitrary"),
            vmem_limit_bytes=VMEM_LIMIT_BYTES),
        name="hybrid_layer",
    )(x, *consts, rope_tab, dtab, dmat)


def _pad_lanes(v, width):
    return jnp.pad(v, (0, width - v.shape[0]))


def _retention_tables(seq, lc):
    half = RET_QK // 2
    inv = ROPE_BASE ** (-jnp.arange(half, dtype=F32) / half)
    ang = jnp.arange(seq).astype(F32)[:, None] * inv
    cos, sin = jnp.cos(ang), jnp.sin(ang)
    cos128 = jnp.tile(cos, (1, LANES // half))
    sin128 = jnp.tile(jnp.concatenate([-sin, sin], axis=1), (1, LANES // RET_QK))
    rope_tab = jnp.concatenate([cos128, sin128], axis=1)
    nv = lc // SUBLANES
    rope_tab = rope_tab.reshape(seq // lc, SUBLANES, nv, 2 * LANES).transpose(0, 2, 1, 3).reshape(seq, 2 * LANES)
    log_g = jnp.log1p(-jnp.exp2(-5.0 - jnp.arange(RET_HEADS, dtype=F32)))
    row = jnp.arange(lc)
    idx = ((row % SUBLANES) * nv + row // SUBLANES).astype(F32)
    rel = idx[:, None] - idx[None, :]
    dmat = jnp.where(rel >= 0, jnp.exp(log_g[:, None, None] * jnp.maximum(rel, 0.0)), 0.0)
    dmat = dmat.transpose(1, 0, 2).reshape(lc, RET_HEADS * lc)
    dec_start = jnp.repeat(jnp.exp(log_g[None, :] * (idx[:, None] + 1.0)), RET_QK, axis=1)
    dec_end = jnp.repeat(jnp.exp(log_g[None, :] * (lc - 1.0 - idx[:, None])), RET_QK, axis=1)
    chunk_dec = jnp.broadcast_to(jnp.exp(log_g * lc)[:, None], (RET_HEADS, 2 * RET_HEADS * RET_QK))
    dtab = jnp.concatenate([jnp.concatenate([dec_start, dec_end], axis=1), chunk_dec,
                            jnp.zeros((SUBLANES - RET_HEADS, 2 * RET_HEADS * RET_QK), F32)], axis=0)
    return rope_tab, dtab, dmat


def _layer_operands(l, norm_g, w_in, a_conv_w, a_conv_b, a_gate_a_w, a_gate_a_b, a_gate_x_w, a_gate_x_b,
                    a_lambda, b_conv_w, b_conv_b, b_dt_bias, b_a_log, b_d_skip, b_norm_g, c_i_bias,
                    c_f_bias, c_norm_g, d_norm_g, w_branch, w_out, final_norm_g):
    w = w_in[l]
    o = 0
    a0 = o; o += 2 * BR
    b0 = o; o += 3 * BR
    dt0 = o; o += SSD_HEADS
    c0 = o; o += 5 * BR
    if0 = o; o += 2 * ML_HEADS
    d0 = o; o += 2 * RET_HEADS * RET_QK + 2 * BR
    g0 = o; o += 4 * D_MODEL
    assert o == w.shape[1]
    bf = lambda t: t.astype(MXU_DTYPE)
    wa, wb, wc, wd, wg = (bf(w[:, a0:b0]), bf(w[:, b0:dt0]), bf(w[:, c0:if0]), bf(w[:, d0:g0]), bf(w[:, g0:o]))
    ws = bf(jnp.concatenate([w[:, dt0:c0], w[:, if0:d0],
                             jnp.zeros((D_MODEL, LANES - SSD_HEADS - 2 * ML_HEADS), w.dtype)], axis=1))
    eye = jnp.eye(RG_BLOCKS, dtype=F32)
    blockdiag = lambda t: (eye[:, None, :, None] * t[:, :, None, :]).reshape(BR, BR)
    agate = bf(jnp.concatenate([blockdiag(a_gate_a_w[l]), blockdiag(a_gate_x_w[l])], axis=1))
    rows = [None] * VEC_ROWS
    rows[R_NORM_G] = norm_g[l]
    rows[R_FINAL_G] = final_norm_g
    for j in range(CONV_W):
        rows[R_A_CONVW + j] = a_conv_w[l, j]
        rows[R_B_CONVW + j] = b_conv_w[l, j]
    rows[R_A_CONVB] = a_conv_b[l]
    rows[R_A_GATEB] = jnp.concatenate([a_gate_a_b[l], a_gate_x_b[l]])
    rows[R_A_LAM] = a_lambda[l]
    rows[R_B_CONVB] = b_conv_b[l]
    rows[R_B_DSKIP] = jnp.repeat(b_d_skip[l], SSD_HEAD_DIM)
    rows[R_B_NORMG] = b_norm_g[l]
    rows[R_C_NORMG] = c_norm_g[l]
    rows[R_D_NORMG] = d_norm_g[l]
    rows[R_B_DTBIAS] = b_dt_bias[l]
    rows[R_B_ALOG] = b_a_log[l]
    rows[R_C_IBIAS] = c_i_bias[l]
    rows[R_C_FBIAS] = c_f_bias[l]
    vecs = jnp.stack([_pad_lanes(r.astype(F32), D_MODEL) if r is not None else jnp.zeros((D_MODEL,), F32)
                      for r in rows])
    return vecs, wa, wb, wc, wd, ws, wg, agate, bf(w_branch[l]), bf(w_out[l])


def kernel(x, norm_g, w_in, a_conv_w, a_conv_b, a_gate_a_w, a_gate_a_b, a_gate_x_w, a_gate_x_b, a_lambda,
           b_conv_w, b_conv_b, b_dt_bias, b_a_log, b_d_skip, b_norm_g, c_i_bias, c_f_bias, c_norm_g,
           d_norm_g, w_branch, w_out, final_norm_g):
    depth = w_in.shape[0]
    seq = x.shape[1]
    ts = min(SEQ_TILE, seq)
    lc = min(CHUNK, ts)
    rope_tab, dtab, dmat = _retention_tables(seq, lc)
    params = (norm_g, w_in, a_conv_w, a_conv_b, a_gate_a_w, a_gate_a_b, a_gate_x_w, a_gate_x_b, a_lambda,
              b_conv_w, b_conv_b, b_dt_bias, b_a_log, b_d_skip, b_norm_g, c_i_bias, c_f_bias, c_norm_g,
              d_norm_g, w_branch, w_out, final_norm_g)
    for l in range(depth):
        ops = _layer_operands(l, *params)
        x = _layer_call(x, *ops, rope_tab, dtab, dmat, ts=ts, lc=lc,
                        first_layer=(l == 0), last_layer=(l == depth - 1))
    return x
```

```python
import functools

import jax
import jax.numpy as jnp
from jax import lax
from jax.experimental import pallas as pl
from jax.experimental.pallas import tpu as pltpu

F32 = jnp.float32
MXU_DTYPE = jnp.bfloat16

D_MODEL = 1024
BR = 512
EPS = 1e-6
CONV_W = 4
RG_BLOCKS = 8
RG_C = 8.0
SSD_HEADS = 8
SSD_GROUPS = 2
SSD_HPG = 4
SSD_HEAD_DIM = 64
SSD_STATE = 128
ML_HEADS = 4
ML_HEAD_DIM = 128
RET_HEADS = 4
RET_QK = 64
RET_V = 128
ROPE_BASE = 10000.0
LANES = 128
SUBLANES = 8

SEQ_TILE = 512
CHUNK = 128
VMEM_LIMIT_BYTES = 60 * 1024 * 1024

(R_NORM_G, R_FINAL_G, R_A_CONVW, R_A_CONVB, R_A_GATEB, R_A_LAM, R_B_CONVW, R_B_CONVB, R_B_DSKIP,
 R_B_NORMG, R_C_NORMG, R_D_NORMG, R_B_DTBIAS, R_B_ALOG, R_C_IBIAS, R_C_FBIAS) = (
    0, 1, 2, 6, 7, 8, 9, 13, 14, 15, 16, 17, 18, 19, 20, 21)
VEC_ROWS = 24

SMALL_DT, SMALL_I, SMALL_F = 0, 8, 12

OFF_A = 0
OFF_S = OFF_A + 2 * BR
OFF_B = OFF_S + LANES
OFF_C = OFF_B + 3 * BR
OFF_D = OFF_C + 5 * BR
OFF_G = OFF_D + 2 * RET_HEADS * RET_QK + 2 * BR
W_COLS = OFF_G + 4 * D_MODEL


def _mm(a, b):
    return jnp.dot(a.astype(MXU_DTYPE), b.astype(MXU_DTYPE), preferred_element_type=F32)


def _mm_nt(a, b):
    return lax.dot_general(a.astype(MXU_DTYPE), b.astype(MXU_DTYPE), (((1,), (1,)), ((), ())),
                           preferred_element_type=F32)


def _mm_tn(a, b):
    return lax.dot_general(a.astype(MXU_DTYPE), b.astype(MXU_DTYPE), (((0,), (0,)), ((), ())),
                           preferred_element_type=F32)


def _sigmoid(x):
    return 0.5 * jnp.tanh(0.5 * x) + 0.5


def _silu(x):
    return x * _sigmoid(x)


def _softplus(x):
    return jnp.maximum(x, 0.0) + jnp.log1p(jnp.exp(-jnp.abs(x)))


def _rmsnorm(x, g):
    return x * lax.rsqrt(jnp.mean(x * x, axis=-1, keepdims=True) + EPS) * g


def _headwise_rmsnorm(y, g, n_heads):
    w = y.shape[1] // n_heads
    parts = []
    for h in range(n_heads):
        yh = y[:, h * w:(h + 1) * w]
        parts.append(yh * lax.rsqrt(jnp.mean(yh * yh, axis=-1, keepdims=True) + EPS))
    return jnp.concatenate(parts, axis=1) * g


def _to_strided(x, lc):
    nv = lc // SUBLANES
    return jnp.concatenate(
        [jnp.swapaxes(x[c * lc:(c + 1) * lc, :].reshape(SUBLANES, nv, x.shape[1]), 0, 1).reshape(lc, x.shape[1])
         for c in range(x.shape[0] // lc)], axis=0)


def _from_strided(x, lc):
    nv = lc // SUBLANES
    return jnp.concatenate(
        [jnp.swapaxes(x[c * lc:(c + 1) * lc, :].reshape(nv, SUBLANES, x.shape[1]), 0, 1).reshape(lc, x.shape[1])
         for c in range(x.shape[0] // lc)], axis=0)


def _strided_time(shape, axis, lc):
    i = lax.broadcasted_iota(jnp.int32, shape, axis)
    return (i & (SUBLANES - 1)) * (lc // SUBLANES) + (i >> 3)


def _sublane(width):
    return lax.broadcasted_iota(jnp.int32, (SUBLANES, width), 0)


def _slabs(v, r0, n):
    return [v[r0 + i * SUBLANES:r0 + (i + 1) * SUBLANES, :] for i in range(n)]


def _cumsum_time(v, lc):
    nv = lc // SUBLANES
    sub = _sublane(v.shape[1])
    out = []
    for c in range(v.shape[0] // lc):
        acc = _slabs(v, c * lc, nv)
        for i in range(1, nv):
            acc[i] = acc[i - 1] + acc[i]
        tot = acc[-1]
        inc = tot
        for k in (1, 2, 4):
            inc = inc + jnp.where(sub >= k, pltpu.roll(inc, k, 0), 0.0)
        out += [a + (inc - tot) for a in acc]
    return jnp.concatenate(out, axis=0)


def _scan_time(a, d, h_prev, lc):
    nv = lc // SUBLANES
    sub = _sublane(a.shape[1])
    out = []
    for c in range(a.shape[0] // lc):
        p, h = _slabs(a, c * lc, nv), _slabs(d, c * lc, nv)
        for i in range(1, nv):
            h[i] = p[i] * h[i - 1] + h[i]
            p[i] = p[i] * p[i - 1]
        pt, ht = p[-1], h[-1]
        for k in (1, 2, 4):
            keep = sub >= k
            ht = pt * jnp.where(keep, pltpu.roll(ht, k, 0), 0.0) + ht
            pt = pt * jnp.where(keep, pltpu.roll(pt, k, 0), 1.0)
        ends = ht + pt * h_prev
        carry = jnp.where(sub == 0, h_prev, pltpu.roll(ends, 1, 0))
        out += [h[i] + p[i] * carry for i in range(nv)]
        h_prev = ends[SUBLANES - 1:SUBLANES, :]
    return jnp.concatenate(out, axis=0), h_prev


def _chunk_last_bcast(v, lc):
    n = v.shape[0] // lc
    return jnp.concatenate(
        [jnp.broadcast_to(v[(c + 1) * lc - 1:(c + 1) * lc, :], (lc, v.shape[1])) for c in range(n)], axis=0)


def _head_bcast(v, n_heads, head_w):
    r = v.shape[0]
    cols = [jnp.broadcast_to(v[:, h:h + 1], (r, LANES)) for h in range(n_heads)]
    if head_w == LANES:
        return jnp.concatenate(cols, axis=1)
    assert head_w * 2 == LANES
    lane = lax.broadcasted_iota(jnp.int32, (r, LANES), 1)
    return jnp.concatenate(
        [jnp.where(lane < head_w, cols[2 * j], cols[2 * j + 1]) for j in range(n_heads // 2)], axis=1)


def _block_rows(x, n_blocks):
    rows, w = x.shape
    bl = w // n_blocks
    n_tiles = w // LANES
    zero = jnp.zeros((rows, LANES), MXU_DTYPE)
    half = lax.broadcasted_iota(jnp.int32, (rows, LANES), 1) // bl if bl < LANES else None
    out = []
    for h in range(n_blocks):
        tiles = []
        for j in range(n_tiles):
            if bl >= LANES:
                keep = (j * LANES) // bl == h
                tiles.append(x[:, j * LANES:(j + 1) * LANES].astype(MXU_DTYPE) if keep else zero)
            elif j == (h * bl) // LANES:
                tiles.append(jnp.where(half == h % (LANES // bl), x[:, j * LANES:(j + 1) * LANES], 0.0)
                             .astype(MXU_DTYPE))
            else:
                tiles.append(zero)
        out.append(jnp.concatenate(tiles, axis=1))
    return jnp.concatenate(out, axis=0)


def _block_diag(blocks):
    n = len(blocks)
    zero = jnp.zeros(blocks[0].shape, MXU_DTYPE)
    return jnp.concatenate(
        [jnp.concatenate([blocks[i].astype(MXU_DTYPE) if j == i else zero for j in range(n)], axis=1)
         for i in range(n)], axis=0)


def _lane_bcast_heads(cols, width):
    return jnp.concatenate([jnp.broadcast_to(c, (c.shape[0], width)) for c in cols], axis=1)


def _causal_conv(tail_ref, x, w, b, lc):
    nt = (CONV_W - 1) * SUBLANES
    sub0 = _sublane(x.shape[1]) == 0
    prev_tail = tail_ref[...]
    out = []
    for c in range(x.shape[0] // lc):
        xc = x[c * lc:(c + 1) * lc, :]
        cur_tail = xc[lc - nt:lc, :]
        wrap = jnp.concatenate(
            [jnp.where(sub0, pltpu.roll(p, 1, 0), pltpu.roll(q, 1, 0))
             for p, q in zip(_slabs(prev_tail, 0, CONV_W - 1), _slabs(cur_tail, 0, CONV_W - 1))], axis=0)
        acc = b + w[CONV_W - 1:CONV_W, :] * xc
        for sh in range(1, CONV_W):
            shifted = jnp.concatenate([wrap[nt - sh * SUBLANES:nt, :], xc[0:lc - sh * SUBLANES, :]], axis=0)
            acc = acc + w[CONV_W - 1 - sh:CONV_W - sh, :] * shifted
        out.append(acc)
        prev_tail = cur_tail
    tail_ref[...] = prev_tail
    return jnp.concatenate(out, axis=0)


def _layer_kernel(x_ref, vec_ref, w_ref, agate_ref, wbr_ref, wout_ref, rope_ref, dtab_ref, dmat_ref, o_ref,
                  a_tail, a_h, b_tail, b_state, c_state, c_n, c_m, d_state,
                  *, ts, lc, first_layer, last_layer):
    nchunk = ts // lc

    @pl.when(pl.program_id(1) == 0)
    def _():
        a_tail[...] = jnp.zeros_like(a_tail)
        a_h[...] = jnp.zeros_like(a_h)
        b_tail[...] = jnp.zeros_like(b_tail)
        b_state[...] = jnp.zeros_like(b_state)
        c_state[...] = jnp.zeros_like(c_state)
        c_n[...] = jnp.zeros_like(c_n)
        c_m[...] = jnp.zeros_like(c_m)
        d_state[...] = jnp.zeros_like(d_state)

    def vec(row, width, nrows=1):
        return vec_ref[row:row + nrows, 0:width]

    half = ts // 2
    x_halves, hn_halves, pa_halves = [], [], []
    for r0 in (0, half):
        xh_ = x_ref[0, r0:r0 + half, :]
        xh_ = _to_strided(xh_, lc) if first_layer else xh_
        hh_ = _rmsnorm(xh_, vec(R_NORM_G, D_MODEL)).astype(MXU_DTYPE)
        x_halves.append(xh_)
        hn_halves.append(hh_)
        pa_halves.append(_mm(hh_, w_ref[:, OFF_A:OFF_B]))
    hn = jnp.concatenate(hn_halves, axis=0)
    pa = jnp.concatenate(pa_halves, axis=0)
    small = pa[:, 2 * BR:2 * BR + LANES]

    def gated_up(n, y):
        gate = _sigmoid(_mm(hn, w_ref[:, OFF_G + n * D_MODEL:OFF_G + (n + 1) * D_MODEL]))
        return gate * _mm(y, wbr_ref[n])

    causal = _strided_time((lc, lc), 0, lc) >= _strided_time((lc, lc), 1, lc)
    neg_inf = jnp.float32(-jnp.inf)

    u, za = pa[:, 0:BR], pa[:, BR:2 * BR]
    xc = _causal_conv(a_tail, u, vec(R_A_CONVW, BR, CONV_W), vec(R_A_CONVB, BR), lc)
    hw = BR // 2
    g_halves = [_mm(xc[:, j * hw:(j + 1) * hw], agate_ref[j]) for j in range(2)]
    gate_b = vec(R_A_GATEB, 2 * BR)
    r_gate = _sigmoid(jnp.concatenate([g[:, 0:hw] for g in g_halves], axis=1) + gate_b[:, 0:BR])
    i_gate = _sigmoid(jnp.concatenate([g[:, hw:2 * hw] for g in g_halves], axis=1) + gate_b[:, BR:2 * BR])
    log_a = (-RG_C) * r_gate * _softplus(-vec(R_A_LAM, BR))
    a = jnp.exp(log_a)
    drive = jnp.sqrt(1.0 - a * a) * (i_gate * xc)
    h, h_last = _scan_time(a, drive, a_h[0:1, :], lc)
    a_h[0:1, :] = h_last
    merged = gated_up(0, h * _silu(za))

    pb = _mm(hn, w_ref[:, OFF_B:OFF_C])
    zb = pb[:, 2 * BR:3 * BR]
    xbc = _silu(_causal_conv(b_tail, pb[:, 0:2 * BR], vec(R_B_CONVW, 2 * BR, CONV_W),
                             vec(R_B_CONVB, 2 * BR), lc))
    xh = xbc[:, 0:BR]
    bm = xbc[:, BR:BR + 2 * SSD_STATE]
    cm = xbc[:, BR + 2 * SSD_STATE:BR + 4 * SSD_STATE]
    dt = _softplus(small + vec(R_B_DTBIAS, LANES))
    a_dt = -jnp.exp(vec(R_B_ALOG, LANES)) * dt
    a_cs = _cumsum_time(a_dt, lc)
    a_tot = _chunk_last_bcast(a_cs, lc)
    dt_f = _head_bcast(dt, SSD_HEADS, SSD_HEAD_DIM)
    eacs_f = _head_bcast(jnp.exp(a_cs), SSD_HEADS, SSD_HEAD_DIM)
    dte_f = _head_bcast(jnp.exp(a_tot - a_cs), SSD_HEADS, SSD_HEAD_DIM)
    x_dt = xh * dt_f
    xs = x_dt * dte_f
    gw = SSD_HPG * SSD_HEAD_DIM
    y_rows = []
    for c in range(nchunk):
        r0, r1 = c * lc, (c + 1) * lc
        acs_c = a_cs[r0:r1, :]
        acs_t = acs_c.T
        c_c, b_c = cm[r0:r1, :], bm[r0:r1, :]
        cb = _mm_nt(c_c, _block_rows(b_c, SSD_GROUPS))
        y_groups = []
        for g in range(SSD_GROUPS):
            cb_g = cb[:, g * lc:(g + 1) * lc]
            masked = []
            for e in range(SSD_HPG):
                hh = g * SSD_HPG + e
                diff = acs_c[:, hh:hh + 1] - acs_t[hh:hh + 1, :]
                decay = jnp.exp(jnp.where(causal, diff, neg_inf))
                masked.append((cb_g * decay).astype(MXU_DTYPE))
            y_groups.append(_mm(jnp.concatenate(masked, axis=1),
                                _block_rows(x_dt[r0:r1, g * gw:(g + 1) * gw], SSD_HPG)))
        states = [b_state[g] for g in range(SSD_GROUPS)]
        y_off = _mm(c_c, _block_diag(states)) * eacs_f[r0:r1, :]
        y_rows.append(jnp.concatenate(y_groups, axis=1) + y_off)
        upd = _mm_tn(b_c, xs[r0:r1, :])
        for g in range(SSD_GROUPS):
            b_state[g] = (eacs_f[r1 - 1:r1, g * gw:(g + 1) * gw] * states[g]
                          + upd[g * SSD_STATE:(g + 1) * SSD_STATE, g * gw:(g + 1) * gw])
    y_b = jnp.concatenate(y_rows, axis=0) + vec(R_B_DSKIP, BR) * xh
    merged = merged + gated_up(1, _rmsnorm(y_b * _silu(zb), vec(R_B_NORMG, BR)))

    pc = _mm(hn, w_ref[:, OFF_C:OFF_D])
    qc = pc[:, 0:BR]
    kc = pc[:, BR:2 * BR] * (ML_HEAD_DIM ** -0.5)
    vc = pc[:, 2 * BR:3 * BR]
    oc = pc[:, 3 * BR:4 * BR]
    zc = pc[:, 4 * BR:5 * BR]
    log_i = pltpu.roll(small, LANES - SMALL_I, 1) + vec(R_C_IBIAS, LANES)
    log_f = -_softplus(-(pltpu.roll(small, LANES - SMALL_F, 1) + vec(R_C_FBIAS, LANES)))
    f_cs = _cumsum_time(log_f, lc)
    f_tot = _chunk_last_bcast(f_cs, lc)
    w_end = f_tot - f_cs + log_i
    g_src = log_i - f_cs
    m_state = c_m[0:1, :]
    h_rows = []
    for c in range(nchunk):
        r0, r1 = c * lc, (c + 1) * lc
        fcs_c = f_cs[r0:r1, :]
        w_c = w_end[r0:r1, :]
        m_loc = jnp.max(w_c, axis=0, keepdims=True)
        p_end = jnp.exp(w_c - m_loc)
        g_t = g_src[r0:r1, :].T
        ft = f_tot[r1 - 1:r1, :]
        m_new = jnp.maximum(ft + m_state, m_loc)
        s_old = jnp.exp(ft + m_state - m_new)
        s_loc = jnp.exp(m_loc - m_new)
        q_c, k_c, v_c = qc[r0:r1, :], kc[r0:r1, :], vc[r0:r1, :]
        qk = _mm_nt(q_c, _block_rows(k_c, ML_HEADS))
        n_prev = c_n[0:1, :]
        qn = q_c * n_prev
        scores, scale_prev, inv = [], [], []
        for hh in range(ML_HEADS):
            fcol = fcs_c[:, hh:hh + 1]
            log_d = jnp.where(causal, fcol + g_t[hh:hh + 1, :], neg_inf)
            log_prev = fcol + m_state[:, hh:hh + 1]
            m_row = jnp.maximum(log_prev, jnp.max(log_d, axis=1, keepdims=True))
            s_h = qk[:, hh * lc:(hh + 1) * lc] * jnp.exp(log_d - m_row)
            sp = jnp.exp(log_prev - m_row)
            den = (jnp.sum(s_h, axis=1, keepdims=True)
                   + sp * jnp.sum(qn[:, hh * ML_HEAD_DIM:(hh + 1) * ML_HEAD_DIM], axis=1, keepdims=True))
            scores.append(s_h.astype(MXU_DTYPE))
            scale_prev.append(sp)
            inv.append(1.0 / jnp.maximum(jnp.abs(den), jnp.exp(-m_row)))
        c_prev = [c_state[hh] for hh in range(ML_HEADS)]
        num = (_mm(jnp.concatenate(scores, axis=1), _block_rows(v_c, ML_HEADS))
               + _lane_bcast_heads(scale_prev, ML_HEAD_DIM) * _mm_nt(q_c, _block_diag(c_prev)))
        h_rows.append(num * _lane_bcast_heads(inv, ML_HEAD_DIM))
        p_f = _lane_bcast_heads([p_end[:, hh:hh + 1] for hh in range(ML_HEADS)], ML_HEAD_DIM)
        c_loc = _mm_tn(p_f * v_c, k_c)
        for hh in range(ML_HEADS):
            d0, d1 = hh * ML_HEAD_DIM, (hh + 1) * ML_HEAD_DIM
            c_state[hh] = s_old[:, hh:hh + 1] * c_prev[hh] + s_loc[:, hh:hh + 1] * c_loc[d0:d1, d0:d1]
        so_f = _lane_bcast_heads([s_old[:, hh:hh + 1] for hh in range(ML_HEADS)], ML_HEAD_DIM)
        sl_f = _lane_bcast_heads([s_loc[:, hh:hh + 1] for hh in range(ML_HEADS)], ML_HEAD_DIM)
        c_n[0:1, :] = so_f * n_prev + sl_f * jnp.sum(p_f * k_c, axis=0, keepdims=True)
        m_state = m_new
    c_m[0:1, :] = m_state
    y_c = _sigmoid(oc) * jnp.concatenate(h_rows, axis=0)
    merged = merged + gated_up(2, _headwise_rmsnorm(y_c, vec(R_C_NORMG, BR), ML_HEADS) * _silu(zc))

    pd = _mm(hn, w_ref[:, OFF_D:OFF_G])
    qk_w = RET_HEADS * RET_QK
    vd = pd[:, 2 * qk_w:2 * qk_w + BR]
    zd = pd[:, 2 * qk_w + BR:2 * qk_w + 2 * BR]
    cos2 = jnp.concatenate([rope_ref[:, 0:LANES]] * 2, axis=1)
    sin2 = jnp.concatenate([rope_ref[:, LANES:2 * LANES]] * 2, axis=1)
    lane_q = lax.broadcasted_iota(jnp.int32, (ts, qk_w), 1)
    first_half = (lane_q & (RET_QK - 1)) < RET_QK // 2

    def rope(t):
        partner = jnp.where(first_half, pltpu.roll(t, qk_w - RET_QK // 2, 1), pltpu.roll(t, RET_QK // 2, 1))
        return t * cos2 + partner * sin2

    qr = rope(pd[:, 0:qk_w])
    kr = rope(pd[:, qk_w:2 * qk_w]) * (RET_QK ** -0.5)
    dec_start = dtab_ref[0:lc, 0:qk_w]
    dec_end = dtab_ref[0:lc, qk_w:2 * qk_w]
    y_rows = []
    for c in range(nchunk):
        r0, r1 = c * lc, (c + 1) * lc
        qr_c, kr_c, v_c = qr[r0:r1, :], kr[r0:r1, :], vd[r0:r1, :]
        inner = _mm_nt(qr_c, _block_rows(kr_c, RET_HEADS)) * dmat_ref[...]
        r_prev = [d_state[hh] for hh in range(RET_HEADS)]
        y_rows.append(_mm(inner, _block_rows(v_c, RET_HEADS))
                      + _mm(qr_c * dec_start, _block_diag(r_prev)))
        upd = _mm_tn(kr_c * dec_end, v_c)
        for hh in range(RET_HEADS):
            chunk_dec = dtab_ref[lc + hh:lc + hh + 1, 0:RET_V]
            d_state[hh] = chunk_dec * r_prev[hh] + upd[hh * RET_QK:(hh + 1) * RET_QK, hh * RET_V:(hh + 1) * RET_V]
    y_d = _headwise_rmsnorm(jnp.concatenate(y_rows, axis=0), vec(R_D_NORMG, BR), RET_HEADS) * _silu(zd)

    merged = merged + gated_up(3, y_d)
    for r0, x_half in zip((0, half), x_halves):
        out = x_half + _mm(merged[r0:r0 + half, :], wout_ref[...])
        if last_layer:
            out = _from_strided(_rmsnorm(out, vec(R_FINAL_G, D_MODEL)), lc)
        o_ref[0, r0:r0 + half, :] = out


def _const_spec(shape):
    zeros = (0,) * len(shape)
    return pl.BlockSpec(shape, lambda b, s: zeros, pipeline_mode=pl.Buffered(1))


def _layer_spec(arr, layer):
    idx = (layer,) + (0,) * (arr.ndim - 1)
    return pl.BlockSpec((None,) + arr.shape[1:], lambda b, s: idx, pipeline_mode=pl.Buffered(1))


def _layer_call(x, layer, vecs, w_all, agate, wbr, wout, rope_tab, dtab, dmat,
                *, ts, lc, first_layer, last_layer):
    bsz, seq, d = x.shape
    assert d == D_MODEL and seq % ts == 0 and ts % (2 * lc) == 0 and lc % LANES == 0
    kern = functools.partial(_layer_kernel, ts=ts, lc=lc, first_layer=first_layer, last_layer=last_layer)
    consts = (vecs, w_all, agate, wbr, wout)
    in_specs = ([pl.BlockSpec((1, ts, d), lambda b, s: (b, s, 0))]
                + [_layer_spec(c, layer) for c in consts]
                + [pl.BlockSpec((ts, 2 * LANES), lambda b, s: (s, 0)),
                   _const_spec(dtab.shape), _const_spec(dmat.shape)])
    scratch = [
        pltpu.VMEM(((CONV_W - 1) * SUBLANES, BR), F32),
        pltpu.VMEM((SUBLANES, BR), F32),
        pltpu.VMEM(((CONV_W - 1) * SUBLANES, 2 * BR), F32),
        pltpu.VMEM((SSD_GROUPS, SSD_STATE, SSD_HPG * SSD_HEAD_DIM), F32),
        pltpu.VMEM((ML_HEADS, ML_HEAD_DIM, ML_HEAD_DIM), F32),
        pltpu.VMEM((SUBLANES, BR), F32),
        pltpu.VMEM((SUBLANES, LANES), F32),
        pltpu.VMEM((RET_HEADS, RET_QK, RET_V), F32),
    ]
    return pl.pallas_call(
        kern,
        grid=(bsz, seq // ts),
        in_specs=in_specs,
        out_specs=pl.BlockSpec((1, ts, d), lambda b, s: (b, s, 0)),
        out_shape=jax.ShapeDtypeStruct(x.shape, x.dtype),
        scratch_shapes=scratch,
        compiler_params=pltpu.CompilerParams(
            dimension_semantics=("arbitrary", "arbitrary"),
            vmem_limit_bytes=VMEM_LIMIT_BYTES),
        name="hybrid_layer",
    )(x, *consts, rope_tab, dtab, dmat)


def _retention_tables(seq, lc):
    half = RET_QK // 2
    nv = lc // SUBLANES
    inv = ROPE_BASE ** (-jnp.arange(half, dtype=F32) / half)
    row = jnp.arange(seq)
    pos = (row // lc) * lc + (row % SUBLANES) * nv + (row % lc) // SUBLANES
    ang = pos.astype(F32)[:, None] * inv
    cos, sin = jnp.cos(ang), jnp.sin(ang)
    rope_tab = jnp.concatenate([cos] * (LANES // half) + [-sin, sin] * (LANES // RET_QK), axis=1)
    log_g = jnp.log1p(-jnp.exp2(-5.0 - jnp.arange(RET_HEADS, dtype=F32)))
    idx = pos[:lc].astype(F32)
    rel = idx[:, None] - idx[None, :]
    dmat = jnp.where(rel >= 0, jnp.exp(log_g[:, None, None] * jnp.maximum(rel, 0.0)), 0.0)
    dmat = dmat.transpose(1, 0, 2).reshape(lc, RET_HEADS * lc)
    dec_start = jnp.repeat(jnp.exp(log_g[None, :] * (idx[:, None] + 1.0)), RET_QK, axis=1)
    dec_end = jnp.repeat(jnp.exp(log_g[None, :] * (lc - 1.0 - idx[:, None])), RET_QK, axis=1)
    chunk_dec = jnp.broadcast_to(jnp.exp(log_g * lc)[:, None], (RET_HEADS, 2 * RET_HEADS * RET_QK))
    dtab = jnp.concatenate([jnp.concatenate([dec_start, dec_end], axis=1), chunk_dec,
                            jnp.zeros((SUBLANES - RET_HEADS, 2 * RET_HEADS * RET_QK), F32)], axis=0)
    return rope_tab, dtab, dmat


def _prepare_operands(norm_g, w_in, a_conv_w, a_conv_b, a_gate_a_w, a_gate_a_b, a_gate_x_w, a_gate_x_b,
                      a_lambda, b_conv_w, b_conv_b, b_dt_bias, b_a_log, b_d_skip, b_norm_g, c_i_bias,
                      c_f_bias, c_norm_g, d_norm_g, w_branch, w_out, final_norm_g):
    depth = w_in.shape[0]
    o = 0
    a0 = o; o += 2 * BR
    b0 = o; o += 3 * BR
    dt0 = o; o += SSD_HEADS
    c0 = o; o += 5 * BR
    if0 = o; o += 2 * ML_HEADS
    d0 = o; o += 2 * RET_HEADS * RET_QK + 2 * BR
    g0 = o; o += 4 * D_MODEL
    assert o == w_in.shape[2]
    w_all = jnp.concatenate(
        [w_in[:, :, a0:b0], w_in[:, :, dt0:c0], w_in[:, :, if0:d0],
         jnp.zeros((depth, D_MODEL, LANES - SSD_HEADS - 2 * ML_HEADS), w_in.dtype),
         w_in[:, :, b0:dt0], w_in[:, :, c0:if0], w_in[:, :, d0:g0], w_in[:, :, g0:o]],
        axis=2).astype(MXU_DTYPE)
    assert w_all.shape[2] == W_COLS

    def gate_blocks(t):
        per_half = RG_BLOCKS // 2
        blk = t.shape[-1]
        rows = [jnp.pad(t[:, h], ((0, 0), (0, 0), ((h % per_half) * blk, (per_half - 1 - h % per_half) * blk)))
                for h in range(RG_BLOCKS)]
        halves = [jnp.concatenate(rows[j * per_half:(j + 1) * per_half], axis=1) for j in range(2)]
        return jnp.stack(halves, axis=1)

    agate = jnp.concatenate([gate_blocks(a_gate_a_w), gate_blocks(a_gate_x_w)], axis=3).astype(MXU_DTYPE)

    def row(v, width=D_MODEL):
        return jnp.pad(v.astype(F32), ((0, 0), (0, width - v.shape[1])))[:, None, :]

    def rows(v):
        return jnp.pad(v.astype(F32), ((0, 0), (0, 0), (0, D_MODEL - v.shape[2])))

    table = {
        R_NORM_G: row(norm_g),
        R_FINAL_G: row(jnp.broadcast_to(final_norm_g, (depth, D_MODEL))),
        R_A_CONVW: rows(a_conv_w),
        R_A_CONVB: row(a_conv_b),
        R_A_GATEB: row(jnp.concatenate([a_gate_a_b, a_gate_x_b], axis=1)),
        R_A_LAM: row(a_lambda),
        R_B_CONVW: rows(b_conv_w),
        R_B_CONVB: row(b_conv_b),
        R_B_DSKIP: row(jnp.repeat(b_d_skip, SSD_HEAD_DIM, axis=1)),
        R_B_NORMG: row(b_norm_g),
        R_C_NORMG: row(c_norm_g),
        R_D_NORMG: row(d_norm_g),
        R_B_DTBIAS: row(b_dt_bias),
        R_B_ALOG: row(b_a_log),
        R_C_IBIAS: row(c_i_bias),
        R_C_FBIAS: row(c_f_bias),
    }
    pieces, r = [], 0
    for start in sorted(table):
        assert start == r
        pieces.append(table[start])
        r += table[start].shape[1]
    pieces.append(jnp.zeros((depth, VEC_ROWS - r, D_MODEL), F32))
    vecs = jnp.concatenate(pieces, axis=1)
    return vecs, w_all, agate, w_branch.astype(MXU_DTYPE), w_out.astype(MXU_DTYPE)


def kernel(x, norm_g, w_in, a_conv_w, a_conv_b, a_gate_a_w, a_gate_a_b, a_gate_x_w, a_gate_x_b, a_lambda,
           b_conv_w, b_conv_b, b_dt_bias, b_a_log, b_d_skip, b_norm_g, c_i_bias, c_f_bias, c_norm_g,
           d_norm_g, w_branch, w_out, final_norm_g):
    depth = w_in.shape[0]
    seq = x.shape[1]
    ts = min(SEQ_TILE, seq)
    lc = min(CHUNK, ts // 2)
    rope_tab, dtab, dmat = _retention_tables(seq, lc)
    ops = _prepare_operands(norm_g, w_in, a_conv_w, a_conv_b, a_gate_a_w, a_gate_a_b, a_gate_x_w, a_gate_x_b,
                            a_lambda, b_conv_w, b_conv_b, b_dt_bias, b_a_log, b_d_skip, b_norm_g, c_i_bias,
                            c_f_bias, c_norm_g, d_norm_g, w_branch, w_out, final_norm_g)
    for l in range(depth):
        x = _layer_call(x, l, *ops, rope_tab, dtab, dmat, ts=ts, lc=lc,
                        first_layer=(l == 0), last_layer=(l == depth - 1))
    return x
```

```python
import functools

import jax
import jax.numpy as jnp
from jax import lax
from jax.experimental import pallas as pl
from jax.experimental.pallas import tpu as pltpu

F32 = jnp.float32
MXU_DTYPE = jnp.bfloat16

D_MODEL = 1024
BR = 512
EPS = 1e-6
CONV_W = 4
RG_BLOCKS = 8
RG_C = 8.0
SSD_HEADS = 8
SSD_GROUPS = 2
SSD_HPG = 4
SSD_HEAD_DIM = 64
SSD_STATE = 128
ML_HEADS = 4
ML_HEAD_DIM = 128
RET_HEADS = 4
RET_QK = 64
RET_V = 128
ROPE_BASE = 10000.0
LANES = 128
SUBLANES = 8

SEQ_TILE = 512
CHUNK = 128
RELAYOUT_ROWS = 128
FILL_COLS = 512
VMEM_LIMIT_BYTES = 60 * 1024 * 1024

(R_NORM_G, R_FINAL_G, R_A_CONVW, R_A_CONVB, R_A_GATEB, R_A_LAM, R_B_CONVW, R_B_CONVB, R_B_DSKIP,
 R_B_NORMG, R_C_NORMG, R_D_NORMG, R_B_DTBIAS, R_B_ALOG, R_C_IBIAS, R_C_FBIAS) = (
    0, 1, 2, 6, 7, 8, 9, 13, 14, 15, 16, 17, 18, 19, 20, 21)
VEC_ROWS = 24

SMALL_DT, SMALL_I, SMALL_F = 0, 8, 12

OFF_A = 0
OFF_S = OFF_A + 2 * BR
OFF_B = OFF_S + LANES
OFF_C = OFF_B + 3 * BR
OFF_D = OFF_C + 5 * BR
OFF_G = OFF_D + 2 * RET_HEADS * RET_QK + 2 * BR
W_COLS = OFF_G + 4 * D_MODEL


def _mm(a, b):
    return jnp.dot(a.astype(MXU_DTYPE), b.astype(MXU_DTYPE), preferred_element_type=F32)


def _mm_nt(a, b):
    return lax.dot_general(a.astype(MXU_DTYPE), b.astype(MXU_DTYPE), (((1,), (1,)), ((), ())),
                           preferred_element_type=F32)


def _mm_tn(a, b):
    return lax.dot_general(a.astype(MXU_DTYPE), b.astype(MXU_DTYPE), (((0,), (0,)), ((), ())),
                           preferred_element_type=F32)


def _sigmoid(x):
    return 0.5 * jnp.tanh(0.5 * x) + 0.5


def _silu(x):
    return x * _sigmoid(x)


def _softplus(x):
    return jnp.maximum(x, 0.0) + jnp.log1p(jnp.exp(-jnp.abs(x)))


def _rmsnorm(x, g):
    return x * lax.rsqrt(jnp.mean(x * x, axis=-1, keepdims=True) + EPS) * g


def _headwise_rmsnorm(y, g, n_heads):
    w = y.shape[1] // n_heads
    parts = []
    for h in range(n_heads):
        yh = y[:, h * w:(h + 1) * w]
        parts.append(yh * lax.rsqrt(jnp.mean(yh * yh, axis=-1, keepdims=True) + EPS))
    return jnp.concatenate(parts, axis=1) * g


def _to_strided(x, lc):
    nv = lc // SUBLANES
    return jnp.concatenate(
        [jnp.swapaxes(x[c * lc:(c + 1) * lc, :].reshape(SUBLANES, nv, x.shape[1]), 0, 1).reshape(lc, x.shape[1])
         for c in range(x.shape[0] // lc)], axis=0)


def _from_strided(x, lc):
    nv = lc // SUBLANES
    return jnp.concatenate(
        [jnp.swapaxes(x[c * lc:(c + 1) * lc, :].reshape(nv, SUBLANES, x.shape[1]), 0, 1).reshape(lc, x.shape[1])
         for c in range(x.shape[0] // lc)], axis=0)


def _strided_time(shape, axis, lc):
    i = lax.broadcasted_iota(jnp.int32, shape, axis)
    return (i & (SUBLANES - 1)) * (lc // SUBLANES) + (i >> 3)


def _sublane(width):
    return lax.broadcasted_iota(jnp.int32, (SUBLANES, width), 0)


def _slabs(v, r0, n):
    return [v[r0 + i * SUBLANES:r0 + (i + 1) * SUBLANES, :] for i in range(n)]


def _cumsum_time(v, lc):
    nv = lc // SUBLANES
    sub = _sublane(v.shape[1])
    out = []
    for c in range(v.shape[0] // lc):
        acc = _slabs(v, c * lc, nv)
        for i in range(1, nv):
            acc[i] = acc[i - 1] + acc[i]
        tot = acc[-1]
        inc = tot
        for k in (1, 2, 4):
            inc = inc + jnp.where(sub >= k, pltpu.roll(inc, k, 0), 0.0)
        out += [a + (inc - tot) for a in acc]
    return jnp.concatenate(out, axis=0)


def _scan_time(a, d, h_prev, lc):
    nv = lc // SUBLANES
    sub = _sublane(a.shape[1])
    out = []
    for c in range(a.shape[0] // lc):
        p, h = _slabs(a, c * lc, nv), _slabs(d, c * lc, nv)
        for i in range(1, nv):
            h[i] = p[i] * h[i - 1] + h[i]
            p[i] = p[i] * p[i - 1]
        pt, ht = p[-1], h[-1]
        for k in (1, 2, 4):
            keep = sub >= k
            ht = pt * jnp.where(keep, pltpu.roll(ht, k, 0), 0.0) + ht
            pt = pt * jnp.where(keep, pltpu.roll(pt, k, 0), 1.0)
        ends = ht + pt * h_prev
        carry = jnp.where(sub == 0, h_prev, pltpu.roll(ends, 1, 0))
        out += [h[i] + p[i] * carry for i in range(nv)]
        h_prev = ends[SUBLANES - 1:SUBLANES, :]
    return jnp.concatenate(out, axis=0), h_prev


def _chunk_last_bcast(v, lc):
    n = v.shape[0] // lc
    return jnp.concatenate(
        [jnp.broadcast_to(v[(c + 1) * lc - 1:(c + 1) * lc, :], (lc, v.shape[1])) for c in range(n)], axis=0)


def _head_bcast(v, n_heads, head_w):
    r = v.shape[0]
    cols = [jnp.broadcast_to(v[:, h:h + 1], (r, LANES)) for h in range(n_heads)]
    if head_w == LANES:
        return jnp.concatenate(cols, axis=1)
    assert head_w * 2 == LANES
    lane = lax.broadcasted_iota(jnp.int32, (r, LANES), 1)
    return jnp.concatenate(
        [jnp.where(lane < head_w, cols[2 * j], cols[2 * j + 1]) for j in range(n_heads // 2)], axis=1)


def _block_rows(x, n_blocks):
    rows, w = x.shape
    bl = w // n_blocks
    n_tiles = w // LANES
    zero = jnp.zeros((rows, LANES), MXU_DTYPE)
    half = lax.broadcasted_iota(jnp.int32, (rows, LANES), 1) // bl if bl < LANES else None
    out = []
    for h in range(n_blocks):
        tiles = []
        for j in range(n_tiles):
            if bl >= LANES:
                keep = (j * LANES) // bl == h
                tiles.append(x[:, j * LANES:(j + 1) * LANES].astype(MXU_DTYPE) if keep else zero)
            elif j == (h * bl) // LANES:
                tiles.append(jnp.where(half == h % (LANES // bl), x[:, j * LANES:(j + 1) * LANES], 0.0)
                             .astype(MXU_DTYPE))
            else:
                tiles.append(zero)
        out.append(jnp.concatenate(tiles, axis=1))
    return jnp.concatenate(out, axis=0)


def _block_diag(blocks):
    n = len(blocks)
    zero = jnp.zeros(blocks[0].shape, MXU_DTYPE)
    return jnp.concatenate(
        [jnp.concatenate([blocks[i].astype(MXU_DTYPE) if j == i else zero for j in range(n)], axis=1)
         for i in range(n)], axis=0)


def _lane_bcast_heads(cols, width):
    return jnp.concatenate([jnp.broadcast_to(c, (c.shape[0], width)) for c in cols], axis=1)


def _causal_conv(tail_ref, x, w, b, lc):
    nt = (CONV_W - 1) * SUBLANES
    sub0 = _sublane(x.shape[1]) == 0
    prev_tail = tail_ref[...]
    out = []
    for c in range(x.shape[0] // lc):
        xc = x[c * lc:(c + 1) * lc, :]
        cur_tail = xc[lc - nt:lc, :]
        wrap = jnp.concatenate(
            [jnp.where(sub0, pltpu.roll(p, 1, 0), pltpu.roll(q, 1, 0))
             for p, q in zip(_slabs(prev_tail, 0, CONV_W - 1), _slabs(cur_tail, 0, CONV_W - 1))], axis=0)
        acc = b + w[CONV_W - 1:CONV_W, :] * xc
        for sh in range(1, CONV_W):
            shifted = jnp.concatenate([wrap[nt - sh * SUBLANES:nt, :], xc[0:lc - sh * SUBLANES, :]], axis=0)
            acc = acc + w[CONV_W - 1 - sh:CONV_W - sh, :] * shifted
        out.append(acc)
        prev_tail = cur_tail
    tail_ref[...] = prev_tail
    return jnp.concatenate(out, axis=0)


def _layer_kernel(x_ref, vec_ref, w_ref, agate_ref, wbr_ref, wout_ref, rope_ref, dtab_ref, dmat_ref, o_ref,
                  a_tail, a_h, b_tail, b_state, c_state, c_n, c_m, d_state,
                  *, ts, lc, first_layer, last_layer):
    nchunk = ts // lc

    @pl.when(pl.program_id(1) == 0)
    def _():
        a_tail[...] = jnp.zeros_like(a_tail)
        a_h[...] = jnp.zeros_like(a_h)
        b_tail[...] = jnp.zeros_like(b_tail)
        b_state[...] = jnp.zeros_like(b_state)
        c_state[...] = jnp.zeros_like(c_state)
        c_n[...] = jnp.zeros_like(c_n)
        c_m[...] = jnp.zeros_like(c_m)
        d_state[...] = jnp.zeros_like(d_state)

    def vec(row, width, nrows=1):
        return vec_ref[row:row + nrows, 0:width]

    half = ts // 2
    x_halves, hn_halves, pa_halves = [], [], []
    for r0 in (0, half):
        xh_ = x_ref[0, r0:r0 + half, :]
        xh_ = _to_strided(xh_, lc) if first_layer else xh_
        hh_ = _rmsnorm(xh_, vec(R_NORM_G, D_MODEL)).astype(MXU_DTYPE)
        x_halves.append(xh_)
        hn_halves.append(hh_)
        pa_halves.append(_mm(hh_, w_ref[:, OFF_A:OFF_B]))
    hn = jnp.concatenate(hn_halves, axis=0)
    pa = jnp.concatenate(pa_halves, axis=0)
    small = pa[:, 2 * BR:2 * BR + LANES]

    pending, pieces = [], {}

    def defer_projection(key, c0, c1):
        pieces[key] = []
        for a in range(c0, c1, FILL_COLS):
            pending.append((key, a, min(a + FILL_COLS, c1)))

    def fill(n=1):
        for _ in range(min(n, len(pending))):
            key, a, b = pending.pop(0)
            pieces[key].append(_mm(hn, w_ref[:, a:b]))

    def take(key):
        while any(k == key for k, _, _ in pending):
            fill()
        return jnp.concatenate(pieces.pop(key), axis=1)

    defer_projection("b", OFF_B, OFF_C)
    defer_projection("g0", OFF_G, OFF_G + D_MODEL)
    defer_projection("c", OFF_C, OFF_D)
    defer_projection("g1", OFF_G + D_MODEL, OFF_G + 2 * D_MODEL)
    defer_projection("d", OFF_D, OFF_G)
    defer_projection("g2", OFF_G + 2 * D_MODEL, OFF_G + 3 * D_MODEL)
    defer_projection("g3", OFF_G + 3 * D_MODEL, OFF_G + 4 * D_MODEL)

    def gated_up(n, y):
        return _sigmoid(take(f"g{n}")) * _mm(y, wbr_ref[n])

    causal = _strided_time((lc, lc), 0, lc) >= _strided_time((lc, lc), 1, lc)
    neg_inf = jnp.float32(-jnp.inf)

    u, za = pa[:, 0:BR], pa[:, BR:2 * BR]
    xc = _causal_conv(a_tail, u, vec(R_A_CONVW, BR, CONV_W), vec(R_A_CONVB, BR), lc)
    hw = BR // 2
    fill(2)
    g_halves = [_mm(xc[:, j * hw:(j + 1) * hw], agate_ref[j]) for j in range(2)]
    fill(2)
    gate_b = vec(R_A_GATEB, 2 * BR)
    r_gate = _sigmoid(jnp.concatenate([g[:, 0:hw] for g in g_halves], axis=1) + gate_b[:, 0:BR])
    i_gate = _sigmoid(jnp.concatenate([g[:, hw:2 * hw] for g in g_halves], axis=1) + gate_b[:, BR:2 * BR])
    log_a = (-RG_C) * r_gate * _softplus(-vec(R_A_LAM, BR))
    a = jnp.exp(log_a)
    drive = jnp.sqrt(1.0 - a * a) * (i_gate * xc)
    h, h_last = _scan_time(a, drive, a_h[0:1, :], lc)
    a_h[0:1, :] = h_last
    merged = gated_up(0, h * _silu(za))

    pb = take("b")
    fill(2)
    zb = pb[:, 2 * BR:3 * BR]
    xbc = _silu(_causal_conv(b_tail, pb[:, 0:2 * BR], vec(R_B_CONVW, 2 * BR, CONV_W),
                             vec(R_B_CONVB, 2 * BR), lc))
    xh = xbc[:, 0:BR]
    bm = xbc[:, BR:BR + 2 * SSD_STATE]
    cm = xbc[:, BR + 2 * SSD_STATE:BR + 4 * SSD_STATE]
    dt = _softplus(small + vec(R_B_DTBIAS, LANES))
    a_dt = -jnp.exp(vec(R_B_ALOG, LANES)) * dt
    a_cs = _cumsum_time(a_dt, lc)
    a_tot = _chunk_last_bcast(a_cs, lc)
    dt_f = _head_bcast(dt, SSD_HEADS, SSD_HEAD_DIM)
    eacs_f = _head_bcast(jnp.exp(a_cs), SSD_HEADS, SSD_HEAD_DIM)
    dte_f = _head_bcast(jnp.exp(a_tot - a_cs), SSD_HEADS, SSD_HEAD_DIM)
    x_dt = xh * dt_f
    xs = x_dt * dte_f
    gw = SSD_HPG * SSD_HEAD_DIM
    y_rows = []
    for c in range(nchunk):
        r0, r1 = c * lc, (c + 1) * lc
        fill()
        acs_c = a_cs[r0:r1, :]
        acs_t = acs_c.T
        c_c, b_c = cm[r0:r1, :], bm[r0:r1, :]
        cb = _mm_nt(c_c, _block_rows(b_c, SSD_GROUPS))
        y_groups = []
        for g in range(SSD_GROUPS):
            cb_g = cb[:, g * lc:(g + 1) * lc]
            masked = []
            for e in range(SSD_HPG):
                hh = g * SSD_HPG + e
                diff = acs_c[:, hh:hh + 1] - acs_t[hh:hh + 1, :]
                decay = jnp.exp(jnp.where(causal, diff, neg_inf))
                masked.append((cb_g * decay).astype(MXU_DTYPE))
            y_groups.append(_mm(jnp.concatenate(masked, axis=1),
                                _block_rows(x_dt[r0:r1, g * gw:(g + 1) * gw], SSD_HPG)))
        states = [b_state[g] for g in range(SSD_GROUPS)]
        y_off = _mm(c_c, _block_diag(states)) * eacs_f[r0:r1, :]
        y_rows.append(jnp.concatenate(y_groups, axis=1) + y_off)
        upd = _mm_tn(b_c, xs[r0:r1, :])
        for g in range(SSD_GROUPS):
            b_state[g] = (eacs_f[r1 - 1:r1, g * gw:(g + 1) * gw] * states[g]
                          + upd[g * SSD_STATE:(g + 1) * SSD_STATE, g * gw:(g + 1) * gw])
    y_b = jnp.concatenate(y_rows, axis=0) + vec(R_B_DSKIP, BR) * xh
    merged = merged + gated_up(1, _rmsnorm(y_b * _silu(zb), vec(R_B_NORMG, BR)))

    pc = take("c")
    fill(2)
    qc = pc[:, 0:BR]
    kc = pc[:, BR:2 * BR] * (ML_HEAD_DIM ** -0.5)
    vc = pc[:, 2 * BR:3 * BR]
    oc = pc[:, 3 * BR:4 * BR]
    zc = pc[:, 4 * BR:5 * BR]
    log_i = pltpu.roll(small, LANES - SMALL_I, 1) + vec(R_C_IBIAS, LANES)
    log_f = -_softplus(-(pltpu.roll(small, LANES - SMALL_F, 1) + vec(R_C_FBIAS, LANES)))
    f_cs = _cumsum_time(log_f, lc)
    f_tot = _chunk_last_bcast(f_cs, lc)
    w_end = f_tot - f_cs + log_i
    g_src = log_i - f_cs
    m_state = c_m[0:1, :]
    h_rows = []
    for c in range(nchunk):
        r0, r1 = c * lc, (c + 1) * lc
        fill()
        fcs_c = f_cs[r0:r1, :]
        w_c = w_end[r0:r1, :]
        m_loc = jnp.max(w_c, axis=0, keepdims=True)
        p_end = jnp.exp(w_c - m_loc)
        g_t = g_src[r0:r1, :].T
        ft = f_tot[r1 - 1:r1, :]
        m_new = jnp.maximum(ft + m_state, m_loc)
        s_old = jnp.exp(ft + m_state - m_new)
        s_loc = jnp.exp(m_loc - m_new)
        q_c, k_c, v_c = qc[r0:r1, :], kc[r0:r1, :], vc[r0:r1, :]
        qk = _mm_nt(q_c, _block_rows(k_c, ML_HEADS))
        n_prev = c_n[0:1, :]
        qn = q_c * n_prev
        scores, scale_prev, inv = [], [], []
        for hh in range(ML_HEADS):
            fcol = fcs_c[:, hh:hh + 1]
            log_d = jnp.where(causal, fcol + g_t[hh:hh + 1, :], neg_inf)
            log_prev = fcol + m_state[:, hh:hh + 1]
            m_row = jnp.maximum(log_prev, jnp.max(log_d, axis=1, keepdims=True))
            s_h = qk[:, hh * lc:(hh + 1) * lc] * jnp.exp(log_d - m_row)
            sp = jnp.exp(log_prev - m_row)
            den = (jnp.sum(s_h, axis=1, keepdims=True)
                   + sp * jnp.sum(qn[:, hh * ML_HEAD_DIM:(hh + 1) * ML_HEAD_DIM], axis=1, keepdims=True))
            scores.append(s_h.astype(MXU_DTYPE))
            scale_prev.append(sp)
            inv.append(1.0 / jnp.maximum(jnp.abs(den), jnp.exp(-m_row)))
        c_prev = [c_state[hh] for hh in range(ML_HEADS)]
        num = (_mm(jnp.concatenate(scores, axis=1), _block_rows(v_c, ML_HEADS))
               + _lane_bcast_heads(scale_prev, ML_HEAD_DIM) * _mm_nt(q_c, _block_diag(c_prev)))
        h_rows.append(num * _lane_bcast_heads(inv, ML_HEAD_DIM))
        p_f = _lane_bcast_heads([p_end[:, hh:hh + 1] for hh in range(ML_HEADS)], ML_HEAD_DIM)
        c_loc = _mm_tn(p_f * v_c, k_c)
        for hh in range(ML_HEADS):
            d0, d1 = hh * ML_HEAD_DIM, (hh + 1) * ML_HEAD_DIM
            c_state[hh] = s_old[:, hh:hh + 1] * c_prev[hh] + s_loc[:, hh:hh + 1] * c_loc[d0:d1, d0:d1]
        so_f = _lane_bcast_heads([s_old[:, hh:hh + 1] for hh in range(ML_HEADS)], ML_HEAD_DIM)
        sl_f = _lane_bcast_heads([s_loc[:, hh:hh + 1] for hh in range(ML_HEADS)], ML_HEAD_DIM)
        c_n[0:1, :] = so_f * n_prev + sl_f * jnp.sum(p_f * k_c, axis=0, keepdims=True)
        m_state = m_new
    c_m[0:1, :] = m_state
    y_c = _sigmoid(oc) * jnp.concatenate(h_rows, axis=0)
    merged = merged + gated_up(2, _headwise_rmsnorm(y_c, vec(R_C_NORMG, BR), ML_HEADS) * _silu(zc))

    pd = take("d")
    fill(2)
    qk_w = RET_HEADS * RET_QK
    vd = pd[:, 2 * qk_w:2 * qk_w + BR]
    zd = pd[:, 2 * qk_w + BR:2 * qk_w + 2 * BR]
    cos2 = jnp.concatenate([rope_ref[:, 0:LANES]] * 2, axis=1)
    sin2 = jnp.concatenate([rope_ref[:, LANES:2 * LANES]] * 2, axis=1)
    lane_q = lax.broadcasted_iota(jnp.int32, (ts, qk_w), 1)
    first_half = (lane_q & (RET_QK - 1)) < RET_QK // 2

    def rope(t):
        partner = jnp.where(first_half, pltpu.roll(t, qk_w - RET_QK // 2, 1), pltpu.roll(t, RET_QK // 2, 1))
        return t * cos2 + partner * sin2

    qr = rope(pd[:, 0:qk_w])
    kr = rope(pd[:, qk_w:2 * qk_w]) * (RET_QK ** -0.5)
    dec_start = dtab_ref[0:lc, 0:qk_w]
    dec_end = dtab_ref[0:lc, qk_w:2 * qk_w]
    y_rows = []
    for c in range(nchunk):
        r0, r1 = c * lc, (c + 1) * lc
        fill()
        qr_c, kr_c, v_c = qr[r0:r1, :], kr[r0:r1, :], vd[r0:r1, :]
        inner = _mm_nt(qr_c, _block_rows(kr_c, RET_HEADS)) * dmat_ref[...]
        r_prev = [d_state[hh] for hh in range(RET_HEADS)]
        y_rows.append(_mm(inner, _block_rows(v_c, RET_HEADS))
                      + _mm(qr_c * dec_start, _block_diag(r_prev)))
        upd = _mm_tn(kr_c * dec_end, v_c)
        for hh in range(RET_HEADS):
            chunk_dec = dtab_ref[lc + hh:lc + hh + 1, 0:RET_V]
            d_state[hh] = chunk_dec * r_prev[hh] + upd[hh * RET_QK:(hh + 1) * RET_QK, hh * RET_V:(hh + 1) * RET_V]
    y_d = _headwise_rmsnorm(jnp.concatenate(y_rows, axis=0), vec(R_D_NORMG, BR), RET_HEADS) * _silu(zd)

    merged = merged + gated_up(3, y_d)
    for r0, x_half in zip((0, half), x_halves):
        out = x_half + _mm(merged[r0:r0 + half, :], wout_ref[...])
        if last_layer:
            out = _from_strided(_rmsnorm(out, vec(R_FINAL_G, D_MODEL)), lc)
        o_ref[0, r0:r0 + half, :] = out


def _const_spec(shape):
    zeros = (0,) * len(shape)
    return pl.BlockSpec(shape, lambda b, s: zeros, pipeline_mode=pl.Buffered(1))


def _layer_spec(arr, layer):
    idx = (layer,) + (0,) * (arr.ndim - 1)
    return pl.BlockSpec((None,) + arr.shape[1:], lambda b, s: idx, pipeline_mode=pl.Buffered(1))


def _layer_call(x, layer, vecs, w_all, agate, wbr, wout, rope_tab, dtab, dmat,
                *, ts, lc, first_layer, last_layer):
    bsz, seq, d = x.shape
    assert d == D_MODEL and seq % ts == 0 and ts % (2 * lc) == 0 and lc % LANES == 0
    kern = functools.partial(_layer_kernel, ts=ts, lc=lc, first_layer=first_layer, last_layer=last_layer)
    consts = (vecs, w_all, agate, wbr, wout)
    in_specs = ([pl.BlockSpec((1, ts, d), lambda b, s: (b, s, 0))]
                + [_layer_spec(c, layer) for c in consts]
                + [pl.BlockSpec((ts, 2 * LANES), lambda b, s: (s, 0)),
                   _const_spec(dtab.shape), _const_spec(dmat.shape)])
    scratch = [
        pltpu.VMEM(((CONV_W - 1) * SUBLANES, BR), F32),
        pltpu.VMEM((SUBLANES, BR), F32),
        pltpu.VMEM(((CONV_W - 1) * SUBLANES, 2 * BR), F32),
        pltpu.VMEM((SSD_GROUPS, SSD_STATE, SSD_HPG * SSD_HEAD_DIM), F32),
        pltpu.VMEM((ML_HEADS, ML_HEAD_DIM, ML_HEAD_DIM), F32),
        pltpu.VMEM((SUBLANES, BR), F32),
        pltpu.VMEM((SUBLANES, LANES), F32),
        pltpu.VMEM((RET_HEADS, RET_QK, RET_V), F32),
    ]
    return pl.pallas_call(
        kern,
        grid=(bsz, seq // ts),
        in_specs=in_specs,
        out_specs=pl.BlockSpec((1, ts, d), lambda b, s: (b, s, 0)),
        out_shape=jax.ShapeDtypeStruct(x.shape, x.dtype),
        scratch_shapes=scratch,
        compiler_params=pltpu.CompilerParams(
            dimension_semantics=("arbitrary", "arbitrary"),
            vmem_limit_bytes=VMEM_LIMIT_BYTES),
        name="hybrid_layer",
    )(x, *consts, rope_tab, dtab, dmat)


def _retention_tables(seq, lc):
    half = RET_QK // 2
    nv = lc // SUBLANES
    inv = ROPE_BASE ** (-jnp.arange(half, dtype=F32) / half)
    row = jnp.arange(seq)
    pos = (row // lc) * lc + (row % SUBLANES) * nv + (row % lc) // SUBLANES
    ang = pos.astype(F32)[:, None] * inv
    cos, sin = jnp.cos(ang), jnp.sin(ang)
    rope_tab = jnp.concatenate([cos] * (LANES // half) + [-sin, sin] * (LANES // RET_QK), axis=1)
    log_g = jnp.log1p(-jnp.exp2(-5.0 - jnp.arange(RET_HEADS, dtype=F32)))
    idx = pos[:lc].astype(F32)
    rel = idx[:, None] - idx[None, :]
    dmat = jnp.where(rel >= 0, jnp.exp(log_g[:, None, None] * jnp.maximum(rel, 0.0)), 0.0)
    dmat = dmat.transpose(1, 0, 2).reshape(lc, RET_HEADS * lc)
    dec_start = jnp.repeat(jnp.exp(log_g[None, :] * (idx[:, None] + 1.0)), RET_QK, axis=1)
    dec_end = jnp.repeat(jnp.exp(log_g[None, :] * (lc - 1.0 - idx[:, None])), RET_QK, axis=1)
    chunk_dec = jnp.broadcast_to(jnp.exp(log_g * lc)[:, None], (RET_HEADS, 2 * RET_HEADS * RET_QK))
    dtab = jnp.concatenate([jnp.concatenate([dec_start, dec_end], axis=1), chunk_dec,
                            jnp.zeros((SUBLANES - RET_HEADS, 2 * RET_HEADS * RET_QK), F32)], axis=0)
    return rope_tab, dtab, dmat


def _relayout_kernel(w_ref, o_ref, *, segments):
    off = 0
    for seg in segments:
        if seg is None:
            width = W_COLS - sum(b - a for a, b in (s for s in segments if s is not None))
            o_ref[:, off:off + width] = jnp.zeros((o_ref.shape[0], width), o_ref.dtype)
        else:
            width = seg[1] - seg[0]
            o_ref[:, off:off + width] = w_ref[:, seg[0]:seg[1]].astype(o_ref.dtype)
        off += width


def _relayout_projection(w_in, segments):
    depth, d, w = w_in.shape
    rows = RELAYOUT_ROWS
    assert d % rows == 0
    return pl.pallas_call(
        functools.partial(_relayout_kernel, segments=segments),
        grid=(depth, d // rows),
        in_specs=[pl.BlockSpec((None, rows, w), lambda l, r: (l, r, 0))],
        out_specs=pl.BlockSpec((None, rows, W_COLS), lambda l, r: (l, r, 0)),
        out_shape=jax.ShapeDtypeStruct((depth, d, W_COLS), MXU_DTYPE),
        compiler_params=pltpu.CompilerParams(dimension_semantics=("arbitrary", "arbitrary"),
                                             vmem_limit_bytes=VMEM_LIMIT_BYTES),
        name="relayout_projection",
    )(w_in)


def _prepare_operands(norm_g, w_in, a_conv_w, a_conv_b, a_gate_a_w, a_gate_a_b, a_gate_x_w, a_gate_x_b,
                      a_lambda, b_conv_w, b_conv_b, b_dt_bias, b_a_log, b_d_skip, b_norm_g, c_i_bias,
                      c_f_bias, c_norm_g, d_norm_g, w_branch, w_out, final_norm_g):
    depth = w_in.shape[0]
    o = 0
    a0 = o; o += 2 * BR
    b0 = o; o += 3 * BR
    dt0 = o; o += SSD_HEADS
    c0 = o; o += 5 * BR
    if0 = o; o += 2 * ML_HEADS
    d0 = o; o += 2 * RET_HEADS * RET_QK + 2 * BR
    g0 = o; o += 4 * D_MODEL
    assert o == w_in.shape[2]
    segments = ((a0, b0), (dt0, c0), (if0, d0), None, (b0, dt0), (c0, if0), (d0, o))
    w_all = _relayout_projection(w_in, segments)

    def gate_blocks(t):
        per_half = RG_BLOCKS // 2
        blk = t.shape[-1]
        rows = [jnp.pad(t[:, h], ((0, 0), (0, 0), ((h % per_half) * blk, (per_half - 1 - h % per_half) * blk)))
                for h in range(RG_BLOCKS)]
        halves = [jnp.concatenate(rows[j * per_half:(j + 1) * per_half], axis=1) for j in range(2)]
        return jnp.stack(halves, axis=1)

    agate = jnp.concatenate([gate_blocks(a_gate_a_w), gate_blocks(a_gate_x_w)], axis=3).astype(MXU_DTYPE)

    def row(v, width=D_MODEL):
        return jnp.pad(v.astype(F32), ((0, 0), (0, width - v.shape[1])))[:, None, :]

    def rows(v):
        return jnp.pad(v.astype(F32), ((0, 0), (0, 0), (0, D_MODEL - v.shape[2])))

    table = {
        R_NORM_G: row(norm_g),
        R_FINAL_G: row(jnp.broadcast_to(final_norm_g, (depth, D_MODEL))),
        R_A_CONVW: rows(a_conv_w),
        R_A_CONVB: row(a_conv_b),
        R_A_GATEB: row(jnp.concatenate([a_gate_a_b, a_gate_x_b], axis=1)),
        R_A_LAM: row(a_lambda),
        R_B_CONVW: rows(b_conv_w),
        R_B_CONVB: row(b_conv_b),
        R_B_DSKIP: row(jnp.repeat(b_d_skip, SSD_HEAD_DIM, axis=1)),
        R_B_NORMG: row(b_norm_g),
        R_C_NORMG: row(c_norm_g),
        R_D_NORMG: row(d_norm_g),
        R_B_DTBIAS: row(b_dt_bias),
        R_B_ALOG: row(b_a_log),
        R_C_IBIAS: row(c_i_bias),
        R_C_FBIAS: row(c_f_bias),
    }
    pieces, r = [], 0
    for start in sorted(table):
        assert start == r
        pieces.append(table[start])
        r += table[start].shape[1]
    pieces.append(jnp.zeros((depth, VEC_ROWS - r, D_MODEL), F32))
    vecs = jnp.concatenate(pieces, axis=1)
    return vecs, w_all, agate, w_branch.astype(MXU_DTYPE), w_out.astype(MXU_DTYPE)


def kernel(x, norm_g, w_in, a_conv_w, a_conv_b, a_gate_a_w, a_gate_a_b, a_gate_x_w, a_gate_x_b, a_lambda,
           b_conv_w, b_conv_b, b_dt_bias, b_a_log, b_d_skip, b_norm_g, c_i_bias, c_f_bias, c_norm_g,
           d_norm_g, w_branch, w_out, final_norm_g):
    depth = w_in.shape[0]
    seq = x.shape[1]
    ts = min(SEQ_TILE, seq)
    lc = min(CHUNK, ts // 2)
    rope_tab, dtab, dmat = _retention_tables(seq, lc)
    ops = _prepare_operands(norm_g, w_in, a_conv_w, a_conv_b, a_gate_a_w, a_gate_a_b, a_gate_x_w, a_gate_x_b,
                            a_lambda, b_conv_w, b_conv_b, b_dt_bias, b_a_log, b_d_skip, b_norm_g, c_i_bias,
                            c_f_bias, c_norm_g, d_norm_g, w_branch, w_out, final_norm_g)
    for l in range(depth):
        x = _layer_call(x, l, *ops, rope_tab, dtab, dmat, ts=ts, lc=lc,
                        first_layer=(l == 0), last_layer=(l == depth - 1))
    return x
```

```python
import functools

import jax
import jax.numpy as jnp
from jax import lax
from jax.experimental import pallas as pl
from jax.experimental.pallas import tpu as pltpu

F32 = jnp.float32
MXU_DTYPE = jnp.bfloat16

D_MODEL = 1024
BR = 512
EPS = 1e-6
CONV_W = 4
RG_BLOCKS = 8
RG_C = 8.0
SSD_HEADS = 8
SSD_GROUPS = 2
SSD_HPG = 4
SSD_HEAD_DIM = 64
SSD_STATE = 128
ML_HEADS = 4
ML_HEAD_DIM = 128
RET_HEADS = 4
RET_QK = 64
RET_V = 128
ROPE_BASE = 10000.0
LANES = 128
SUBLANES = 8

SEQ_TILE = 256
CHUNK = 128
RELAYOUT_ROWS = 128
FILL_AFTER_GATES = 9
FILL_COLS = 512
VMEM_LIMIT_BYTES = 60 * 1024 * 1024

(R_NORM_G, R_FINAL_G, R_A_CONVW, R_A_CONVB, R_A_GATEB, R_A_LAM, R_B_CONVW, R_B_CONVB, R_B_DSKIP,
 R_B_NORMG, R_C_NORMG, R_D_NORMG, R_B_DTBIAS, R_B_ALOG, R_C_IBIAS, R_C_FBIAS) = (
    0, 1, 2, 6, 7, 8, 9, 13, 14, 15, 16, 17, 18, 19, 20, 21)
VEC_ROWS = 24

SMALL_DT, SMALL_I, SMALL_F = 0, 8, 12

OFF_A = 0
OFF_S = OFF_A + 2 * BR
OFF_B = OFF_S + LANES
OFF_C = OFF_B + 3 * BR
OFF_D = OFF_C + 5 * BR
OFF_G = OFF_D + 2 * RET_HEADS * RET_QK + 2 * BR
W_COLS = OFF_G + 4 * D_MODEL


def _mm(a, b):
    return jnp.dot(a.astype(MXU_DTYPE), b.astype(MXU_DTYPE), preferred_element_type=F32)


def _mm_nt(a, b):
    return lax.dot_general(a.astype(MXU_DTYPE), b.astype(MXU_DTYPE), (((1,), (1,)), ((), ())),
                           preferred_element_type=F32)


def _mm_tn(a, b):
    return lax.dot_general(a.astype(MXU_DTYPE), b.astype(MXU_DTYPE), (((0,), (0,)), ((), ())),
                           preferred_element_type=F32)


def _sigmoid(x):
    return 0.5 * jnp.tanh(0.5 * x) + 0.5


def _silu(x):
    return x * _sigmoid(x)


def _softplus(x):
    return jnp.maximum(x, 0.0) + jnp.log1p(jnp.exp(-jnp.abs(x)))


def _rmsnorm(x, g):
    return x * lax.rsqrt(jnp.mean(x * x, axis=-1, keepdims=True) + EPS) * g


def _headwise_rmsnorm(y, g, n_heads):
    w = y.shape[1] // n_heads
    parts = []
    for h in range(n_heads):
        yh = y[:, h * w:(h + 1) * w]
        parts.append(yh * lax.rsqrt(jnp.mean(yh * yh, axis=-1, keepdims=True) + EPS))
    return jnp.concatenate(parts, axis=1) * g


def _to_strided(x, lc):
    nv = lc // SUBLANES
    return jnp.concatenate(
        [jnp.swapaxes(x[c * lc:(c + 1) * lc, :].reshape(SUBLANES, nv, x.shape[1]), 0, 1).reshape(lc, x.shape[1])
         for c in range(x.shape[0] // lc)], axis=0)


def _from_strided(x, lc):
    nv = lc // SUBLANES
    return jnp.concatenate(
        [jnp.swapaxes(x[c * lc:(c + 1) * lc, :].reshape(nv, SUBLANES, x.shape[1]), 0, 1).reshape(lc, x.shape[1])
         for c in range(x.shape[0] // lc)], axis=0)


def _strided_time(shape, axis, lc):
    i = lax.broadcasted_iota(jnp.int32, shape, axis)
    return (i & (SUBLANES - 1)) * (lc // SUBLANES) + (i >> 3)


def _sublane(width):
    return lax.broadcasted_iota(jnp.int32, (SUBLANES, width), 0)


def _slabs(v, r0, n):
    return [v[r0 + i * SUBLANES:r0 + (i + 1) * SUBLANES, :] for i in range(n)]


def _cumsum_time(v, lc):
    nv = lc // SUBLANES
    sub = _sublane(v.shape[1])
    out = []
    for c in range(v.shape[0] // lc):
        acc = _slabs(v, c * lc, nv)
        for i in range(1, nv):
            acc[i] = acc[i - 1] + acc[i]
        tot = acc[-1]
        inc = tot
        for k in (1, 2, 4):
            inc = inc + jnp.where(sub >= k, pltpu.roll(inc, k, 0), 0.0)
        out += [a + (inc - tot) for a in acc]
    return jnp.concatenate(out, axis=0)


def _scan_time(a, d, h_prev, lc):
    nv = lc // SUBLANES
    sub = _sublane(a.shape[1])
    out = []
    for c in range(a.shape[0] // lc):
        p, h = _slabs(a, c * lc, nv), _slabs(d, c * lc, nv)
        for i in range(1, nv):
            h[i] = p[i] * h[i - 1] + h[i]
            p[i] = p[i] * p[i - 1]
        pt, ht = p[-1], h[-1]
        for k in (1, 2, 4):
            keep = sub >= k
            ht = pt * jnp.where(keep, pltpu.roll(ht, k, 0), 0.0) + ht
            pt = pt * jnp.where(keep, pltpu.roll(pt, k, 0), 1.0)
        ends = ht + pt * h_prev
        carry = jnp.where(sub == 0, h_prev, pltpu.roll(ends, 1, 0))
        out += [h[i] + p[i] * carry for i in range(nv)]
        h_prev = ends[SUBLANES - 1:SUBLANES, :]
    return jnp.concatenate(out, axis=0), h_prev


def _chunk_last_bcast(v, lc):
    n = v.shape[0] // lc
    return jnp.concatenate(
        [jnp.broadcast_to(v[(c + 1) * lc - 1:(c + 1) * lc, :], (lc, v.shape[1])) for c in range(n)], axis=0)


def _head_bcast(v, n_heads, head_w):
    r = v.shape[0]
    cols = [jnp.broadcast_to(v[:, h:h + 1], (r, LANES)) for h in range(n_heads)]
    if head_w == LANES:
        return jnp.concatenate(cols, axis=1)
    assert head_w * 2 == LANES
    lane = lax.broadcasted_iota(jnp.int32, (r, LANES), 1)
    return jnp.concatenate(
        [jnp.where(lane < head_w, cols[2 * j], cols[2 * j + 1]) for j in range(n_heads // 2)], axis=1)


def _block_rows(x, n_blocks):
    rows, w = x.shape
    bl = w // n_blocks
    n_tiles = w // LANES
    zero = jnp.zeros((rows, LANES), MXU_DTYPE)
    half = lax.broadcasted_iota(jnp.int32, (rows, LANES), 1) // bl if bl < LANES else None
    out = []
    for h in range(n_blocks):
        tiles = []
        for j in range(n_tiles):
            if bl >= LANES:
                keep = (j * LANES) // bl == h
                tiles.append(x[:, j * LANES:(j + 1) * LANES].astype(MXU_DTYPE) if keep else zero)
            elif j == (h * bl) // LANES:
                tiles.append(jnp.where(half == h % (LANES // bl), x[:, j * LANES:(j + 1) * LANES], 0.0)
                             .astype(MXU_DTYPE))
            else:
                tiles.append(zero)
        out.append(jnp.concatenate(tiles, axis=1))
    return jnp.concatenate(out, axis=0)


def _block_diag(blocks):
    n = len(blocks)
    zero = jnp.zeros(blocks[0].shape, MXU_DTYPE)
    return jnp.concatenate(
        [jnp.concatenate([blocks[i].astype(MXU_DTYPE) if j == i else zero for j in range(n)], axis=1)
         for i in range(n)], axis=0)


def _lane_bcast_heads(cols, width):
    return jnp.concatenate([jnp.broadcast_to(c, (c.shape[0], width)) for c in cols], axis=1)


def _causal_conv(tail_ref, x, w, b, lc):
    nt = (CONV_W - 1) * SUBLANES
    sub0 = _sublane(x.shape[1]) == 0
    prev_tail = tail_ref[...]
    out = []
    for c in range(x.shape[0] // lc):
        xc = x[c * lc:(c + 1) * lc, :]
        cur_tail = xc[lc - nt:lc, :]
        wrap = jnp.concatenate(
            [jnp.where(sub0, pltpu.roll(p, 1, 0), pltpu.roll(q, 1, 0))
             for p, q in zip(_slabs(prev_tail, 0, CONV_W - 1), _slabs(cur_tail, 0, CONV_W - 1))], axis=0)
        acc = b + w[CONV_W - 1:CONV_W, :] * xc
        for sh in range(1, CONV_W):
            shifted = jnp.concatenate([wrap[nt - sh * SUBLANES:nt, :], xc[0:lc - sh * SUBLANES, :]], axis=0)
            acc = acc + w[CONV_W - 1 - sh:CONV_W - sh, :] * shifted
        out.append(acc)
        prev_tail = cur_tail
    tail_ref[...] = prev_tail
    return jnp.concatenate(out, axis=0)


def _layer_kernel(x_ref, vec_ref, w_ref, agate_ref, wbr_ref, wout_ref, rope_ref, dtab_ref, dmat_ref, o_ref,
                  a_tail, a_h, b_tail, b_state, c_state, c_n, c_m, d_state,
                  *, ts, lc, first_layer, last_layer):
    nchunk = ts // lc

    @pl.when(pl.program_id(1) == 0)
    def _():
        a_tail[...] = jnp.zeros_like(a_tail)
        a_h[...] = jnp.zeros_like(a_h)
        b_tail[...] = jnp.zeros_like(b_tail)
        b_state[...] = jnp.zeros_like(b_state)
        c_state[...] = jnp.zeros_like(c_state)
        c_n[...] = jnp.zeros_like(c_n)
        c_m[...] = jnp.zeros_like(c_m)
        d_state[...] = jnp.zeros_like(d_state)

    def vec(row, width, nrows=1):
        return vec_ref[row:row + nrows, 0:width]

    half = ts // 2
    x_halves, hn_halves, pa_halves = [], [], []
    for r0 in (0, half):
        xh_ = x_ref[0, r0:r0 + half, :]
        xh_ = _to_strided(xh_, lc) if first_layer else xh_
        hh_ = _rmsnorm(xh_, vec(R_NORM_G, D_MODEL)).astype(MXU_DTYPE)
        x_halves.append(xh_)
        hn_halves.append(hh_)
        pa_halves.append(_mm(hh_, w_ref[:, OFF_A:OFF_B]))
    hn = jnp.concatenate(hn_halves, axis=0)
    pa = jnp.concatenate(pa_halves, axis=0)
    small = pa[:, 2 * BR:2 * BR + LANES]

    pending, pieces = [], {}

    def defer_projection(key, c0, c1):
        pieces[key] = []
        for a in range(c0, c1, FILL_COLS):
            pending.append((key, a, min(a + FILL_COLS, c1)))

    def fill(n=1):
        for _ in range(min(n, len(pending))):
            key, a, b = pending.pop(0)
            pieces[key].append(_mm(hn, w_ref[:, a:b]))

    def take(key):
        while any(k == key for k, _, _ in pending):
            fill()
        return jnp.concatenate(pieces.pop(key), axis=1)

    defer_projection("b", OFF_B, OFF_C)
    defer_projection("c", OFF_C, OFF_D)
    defer_projection("d", OFF_D, OFF_G)
    for n in range(4):
        defer_projection(f"g{n}", OFF_G + n * D_MODEL, OFF_G + (n + 1) * D_MODEL)
    branch_out = {}

    def gated_up(n, y):
        return _sigmoid(take(f"g{n}")) * _mm(y, wbr_ref[n])

    causal = _strided_time((lc, lc), 0, lc) >= _strided_time((lc, lc), 1, lc)
    neg_inf = jnp.float32(-jnp.inf)

    u, za = pa[:, 0:BR], pa[:, BR:2 * BR]
    xc = _causal_conv(a_tail, u, vec(R_A_CONVW, BR, CONV_W), vec(R_A_CONVB, BR), lc)
    hw = BR // 2
    fill(2)
    g_halves = [_mm(xc[:, j * hw:(j + 1) * hw], agate_ref[j]) for j in range(2)]
    fill(FILL_AFTER_GATES)
    gate_b = vec(R_A_GATEB, 2 * BR)
    r_gate = _sigmoid(jnp.concatenate([g[:, 0:hw] for g in g_halves], axis=1) + gate_b[:, 0:BR])
    i_gate = _sigmoid(jnp.concatenate([g[:, hw:2 * hw] for g in g_halves], axis=1) + gate_b[:, BR:2 * BR])
    log_a = (-RG_C) * r_gate * _softplus(-vec(R_A_LAM, BR))
    a = jnp.exp(log_a)
    drive = jnp.sqrt(1.0 - a * a) * (i_gate * xc)
    h, h_last = _scan_time(a, drive, a_h[0:1, :], lc)
    a_h[0:1, :] = h_last
    merged = gated_up(0, h * _silu(za))

    def ssd_mixer():
        pb = take("b")
        zb = pb[:, 2 * BR:3 * BR]
        xbc = _silu(_causal_conv(b_tail, pb[:, 0:2 * BR], vec(R_B_CONVW, 2 * BR, CONV_W),
                                 vec(R_B_CONVB, 2 * BR), lc))
        xh = xbc[:, 0:BR]
        bm = xbc[:, BR:BR + 2 * SSD_STATE]
        cm = xbc[:, BR + 2 * SSD_STATE:BR + 4 * SSD_STATE]
        dt = _softplus(small + vec(R_B_DTBIAS, LANES))
        a_dt = -jnp.exp(vec(R_B_ALOG, LANES)) * dt
        a_cs = _cumsum_time(a_dt, lc)
        a_tot = _chunk_last_bcast(a_cs, lc)
        dt_f = _head_bcast(dt, SSD_HEADS, SSD_HEAD_DIM)
        eacs_f = _head_bcast(jnp.exp(a_cs), SSD_HEADS, SSD_HEAD_DIM)
        dte_f = _head_bcast(jnp.exp(a_tot - a_cs), SSD_HEADS, SSD_HEAD_DIM)
        x_dt = xh * dt_f
        xs = x_dt * dte_f
        gw = SSD_HPG * SSD_HEAD_DIM
        y_rows = []
        for c in range(nchunk):
            r0, r1 = c * lc, (c + 1) * lc
            yield
            acs_c = a_cs[r0:r1, :]
            acs_t = acs_c.T
            c_c, b_c = cm[r0:r1, :], bm[r0:r1, :]
            cb = _mm_nt(c_c, _block_rows(b_c, SSD_GROUPS))
            yield
            y_groups = []
            for g in range(SSD_GROUPS):
                cb_g = cb[:, g * lc:(g + 1) * lc]
                masked = []
                for e in range(SSD_HPG):
                    hh = g * SSD_HPG + e
                    diff = acs_c[:, hh:hh + 1] - acs_t[hh:hh + 1, :]
                    decay = jnp.exp(jnp.where(causal, diff, neg_inf))
                    masked.append((cb_g * decay).astype(MXU_DTYPE))
                y_groups.append(_mm(jnp.concatenate(masked, axis=1),
                                    _block_rows(x_dt[r0:r1, g * gw:(g + 1) * gw], SSD_HPG)))
            states = [b_state[g] for g in range(SSD_GROUPS)]
            y_off = _mm(c_c, _block_diag(states)) * eacs_f[r0:r1, :]
            y_rows.append(jnp.concatenate(y_groups, axis=1) + y_off)
            upd = _mm_tn(b_c, xs[r0:r1, :])
            for g in range(SSD_GROUPS):
                b_state[g] = (eacs_f[r1 - 1:r1, g * gw:(g + 1) * gw] * states[g]
                              + upd[g * SSD_STATE:(g + 1) * SSD_STATE, g * gw:(g + 1) * gw])
        y_b = jnp.concatenate(y_rows, axis=0) + vec(R_B_DSKIP, BR) * xh
        branch_out[1] = _rmsnorm(y_b * _silu(zb), vec(R_B_NORMG, BR))

    def mlstm_mixer():
        pc = take("c")
        qc = pc[:, 0:BR]
        kc = pc[:, BR:2 * BR] * (ML_HEAD_DIM ** -0.5)
        vc = pc[:, 2 * BR:3 * BR]
        oc = pc[:, 3 * BR:4 * BR]
        zc = pc[:, 4 * BR:5 * BR]
        log_i = pltpu.roll(small, LANES - SMALL_I, 1) + vec(R_C_IBIAS, LANES)
        log_f = -_softplus(-(pltpu.roll(small, LANES - SMALL_F, 1) + vec(R_C_FBIAS, LANES)))
        f_cs = _cumsum_time(log_f, lc)
        f_tot = _chunk_last_bcast(f_cs, lc)
        w_end = f_tot - f_cs + log_i
        g_src = log_i - f_cs
        m_state = c_m[0:1, :]
        h_rows = []
        for c in range(nchunk):
            r0, r1 = c * lc, (c + 1) * lc
            yield
            fcs_c = f_cs[r0:r1, :]
            w_c = w_end[r0:r1, :]
            m_loc = jnp.max(w_c, axis=0, keepdims=True)
            p_end = jnp.exp(w_c - m_loc)
            g_t = g_src[r0:r1, :].T
            ft = f_tot[r1 - 1:r1, :]
            m_new = jnp.maximum(ft + m_state, m_loc)
            s_old = jnp.exp(ft + m_state - m_new)
            s_loc = jnp.exp(m_loc - m_new)
            q_c, k_c, v_c = qc[r0:r1, :], kc[r0:r1, :], vc[r0:r1, :]
            qk = _mm_nt(q_c, _block_rows(k_c, ML_HEADS))
            yield
            n_prev = c_n[0:1, :]
            qn = q_c * n_prev
            scores, scale_prev, inv = [], [], []
            for hh in range(ML_HEADS):
                fcol = fcs_c[:, hh:hh + 1]
                log_d = jnp.where(causal, fcol + g_t[hh:hh + 1, :], neg_inf)
                log_prev = fcol + m_state[:, hh:hh + 1]
                m_row = jnp.maximum(log_prev, jnp.max(log_d, axis=1, keepdims=True))
                s_h = qk[:, hh * lc:(hh + 1) * lc] * jnp.exp(log_d - m_row)
                sp = jnp.exp(log_prev - m_row)
                den = (jnp.sum(s_h, axis=1, keepdims=True)
                       + sp * jnp.sum(qn[:, hh * ML_HEAD_DIM:(hh + 1) * ML_HEAD_DIM], axis=1, keepdims=True))
                scores.append(s_h.astype(MXU_DTYPE))
                scale_prev.append(sp)
                inv.append(1.0 / jnp.maximum(jnp.abs(den), jnp.exp(-m_row)))
            c_prev = [c_state[hh] for hh in range(ML_HEADS)]
            num = (_mm(jnp.concatenate(scores, axis=1), _block_rows(v_c, ML_HEADS))
                   + _lane_bcast_heads(scale_prev, ML_HEAD_DIM) * _mm_nt(q_c, _block_diag(c_prev)))
            h_rows.append(num * _lane_bcast_heads(inv, ML_HEAD_DIM))
            p_f = _lane_bcast_heads([p_end[:, hh:hh + 1] for hh in range(ML_HEADS)], ML_HEAD_DIM)
            c_loc = _mm_tn(p_f * v_c, k_c)
            for hh in range(ML_HEADS):
                d0, d1 = hh * ML_HEAD_DIM, (hh + 1) * ML_HEAD_DIM
                c_state[hh] = s_old[:, hh:hh + 1] * c_prev[hh] + s_loc[:, hh:hh + 1] * c_loc[d0:d1, d0:d1]
            so_f = _lane_bcast_heads([s_old[:, hh:hh + 1] for hh in range(ML_HEADS)], ML_HEAD_DIM)
            sl_f = _lane_bcast_heads([s_loc[:, hh:hh + 1] for hh in range(ML_HEADS)], ML_HEAD_DIM)
            c_n[0:1, :] = so_f * n_prev + sl_f * jnp.sum(p_f * k_c, axis=0, keepdims=True)
            m_state = m_new
        c_m[0:1, :] = m_state
        y_c = _sigmoid(oc) * jnp.concatenate(h_rows, axis=0)
        branch_out[2] = _headwise_rmsnorm(y_c, vec(R_C_NORMG, BR), ML_HEADS) * _silu(zc)

    def retention_mixer():
        pd = take("d")
        qk_w = RET_HEADS * RET_QK
        vd = pd[:, 2 * qk_w:2 * qk_w + BR]
        zd = pd[:, 2 * qk_w + BR:2 * qk_w + 2 * BR]
        cos2 = jnp.concatenate([rope_ref[:, 0:LANES]] * 2, axis=1)
        sin2 = jnp.concatenate([rope_ref[:, LANES:2 * LANES]] * 2, axis=1)
        lane_q = lax.broadcasted_iota(jnp.int32, (ts, qk_w), 1)
        first_half = (lane_q & (RET_QK - 1)) < RET_QK // 2

        def rope(t):
            partner = jnp.where(first_half, pltpu.roll(t, qk_w - RET_QK // 2, 1), pltpu.roll(t, RET_QK // 2, 1))
            return t * cos2 + partner * sin2

        qr = rope(pd[:, 0:qk_w])
        kr = rope(pd[:, qk_w:2 * qk_w]) * (RET_QK ** -0.5)
        dec_start = dtab_ref[0:lc, 0:qk_w]
        dec_end = dtab_ref[0:lc, qk_w:2 * qk_w]
        y_rows = []
        for c in range(nchunk):
            r0, r1 = c * lc, (c + 1) * lc
            yield
            qr_c, kr_c, v_c = qr[r0:r1, :], kr[r0:r1, :], vd[r0:r1, :]
            inner = _mm_nt(qr_c, _block_rows(kr_c, RET_HEADS)) * dmat_ref[...]
            yield
            r_prev = [d_state[hh] for hh in range(RET_HEADS)]
            y_rows.append(_mm(inner, _block_rows(v_c, RET_HEADS))
                          + _mm(qr_c * dec_start, _block_diag(r_prev)))
            upd = _mm_tn(kr_c * dec_end, v_c)
            for hh in range(RET_HEADS):
                chunk_dec = dtab_ref[lc + hh:lc + hh + 1, 0:RET_V]
                d_state[hh] = chunk_dec * r_prev[hh] + upd[hh * RET_QK:(hh + 1) * RET_QK, hh * RET_V:(hh + 1) * RET_V]
        y_d = jnp.concatenate(y_rows, axis=0)
        branch_out[3] = _headwise_rmsnorm(y_d, vec(R_D_NORMG, BR), RET_HEADS) * _silu(zd)

    mixers = [ssd_mixer(), mlstm_mixer(), retention_mixer()]
    while mixers:
        for m in list(mixers):
            if next(m, "done") == "done":
                mixers.remove(m)
        fill()
    for n in (1, 2, 3):
        merged = merged + gated_up(n, branch_out[n])
    for r0, x_half in zip((0, half), x_halves):
        out = x_half + _mm(merged[r0:r0 + half, :], wout_ref[...])
        if last_layer:
            out = _from_strided(_rmsnorm(out, vec(R_FINAL_G, D_MODEL)), lc)
        o_ref[0, r0:r0 + half, :] = out


def _const_spec(shape):
    zeros = (0,) * len(shape)
    return pl.BlockSpec(shape, lambda b, s: zeros, pipeline_mode=pl.Buffered(1))


def _layer_spec(arr, layer):
    idx = (layer,) + (0,) * (arr.ndim - 1)
    return pl.BlockSpec((None,) + arr.shape[1:], lambda b, s: idx, pipeline_mode=pl.Buffered(1))


def _layer_call(x, layer, vecs, w_all, agate, wbr, wout, rope_tab, dtab, dmat,
                *, ts, lc, first_layer, last_layer):
    bsz, seq, d = x.shape
    assert d == D_MODEL and seq % ts == 0 and ts % (2 * lc) == 0 and lc % LANES == 0
    kern = functools.partial(_layer_kernel, ts=ts, lc=lc, first_layer=first_layer, last_layer=last_layer)
    consts = (vecs, w_all, agate, wbr, wout)
    in_specs = ([pl.BlockSpec((1, ts, d), lambda b, s: (b, s, 0))]
                + [_layer_spec(c, layer) for c in consts]
                + [pl.BlockSpec((ts, 2 * LANES), lambda b, s: (s, 0)),
                   _const_spec(dtab.shape), _const_spec(dmat.shape)])
    scratch = [
        pltpu.VMEM(((CONV_W - 1) * SUBLANES, BR), F32),
        pltpu.VMEM((SUBLANES, BR), F32),
        pltpu.VMEM(((CONV_W - 1) * SUBLANES, 2 * BR), F32),
        pltpu.VMEM((SSD_GROUPS, SSD_STATE, SSD_HPG * SSD_HEAD_DIM), F32),
        pltpu.VMEM((ML_HEADS, ML_HEAD_DIM, ML_HEAD_DIM), F32),
        pltpu.VMEM((SUBLANES, BR), F32),
        pltpu.VMEM((SUBLANES, LANES), F32),
        pltpu.VMEM((RET_HEADS, RET_QK, RET_V), F32),
    ]
    return pl.pallas_call(
        kern,
        grid=(bsz, seq // ts),
        in_specs=in_specs,
        out_specs=pl.BlockSpec((1, ts, d), lambda b, s: (b, s, 0)),
        out_shape=jax.ShapeDtypeStruct(x.shape, x.dtype),
        scratch_shapes=scratch,
        compiler_params=pltpu.CompilerParams(
            dimension_semantics=("arbitrary", "arbitrary"),
            vmem_limit_bytes=VMEM_LIMIT_BYTES),
        name="hybrid_layer",
    )(x, *consts, rope_tab, dtab, dmat)


def _retention_tables(seq, lc):
    half = RET_QK // 2
    nv = lc // SUBLANES
    inv = ROPE_BASE ** (-jnp.arange(half, dtype=F32) / half)
    row = jnp.arange(seq)
    pos = (row // lc) * lc + (row % SUBLANES) * nv + (row % lc) // SUBLANES
    ang = pos.astype(F32)[:, None] * inv
    cos, sin = jnp.cos(ang), jnp.sin(ang)
    rope_tab = jnp.concatenate([cos] * (LANES // half) + [-sin, sin] * (LANES // RET_QK), axis=1)
    log_g = jnp.log1p(-jnp.exp2(-5.0 - jnp.arange(RET_HEADS, dtype=F32)))
    idx = pos[:lc].astype(F32)
    rel = idx[:, None] - idx[None, :]
    dmat = jnp.where(rel >= 0, jnp.exp(log_g[:, None, None] * jnp.maximum(rel, 0.0)), 0.0)
    dmat = dmat.transpose(1, 0, 2).reshape(lc, RET_HEADS * lc)
    dec_start = jnp.repeat(jnp.exp(log_g[None, :] * (idx[:, None] + 1.0)), RET_QK, axis=1)
    dec_end = jnp.repeat(jnp.exp(log_g[None, :] * (lc - 1.0 - idx[:, None])), RET_QK, axis=1)
    chunk_dec = jnp.broadcast_to(jnp.exp(log_g * lc)[:, None], (RET_HEADS, 2 * RET_HEADS * RET_QK))
    dtab = jnp.concatenate([jnp.concatenate([dec_start, dec_end], axis=1), chunk_dec,
                            jnp.zeros((SUBLANES - RET_HEADS, 2 * RET_HEADS * RET_QK), F32)], axis=0)
    return rope_tab, dtab, dmat


def _relayout_kernel(wt_ref, o_ref, *, segments):
    narrow = [s for s in segments if s is None or s[1] - s[0] < LANES]
    parts = [wt_ref[s[0]:s[1], :] for s in narrow if s is not None]
    used = sum(p.shape[0] for p in parts)
    parts.append(jnp.zeros((LANES - used, wt_ref.shape[1]), F32))
    narrow_tile = jnp.concatenate(parts, axis=0).T.astype(o_ref.dtype)
    off, narrow_done = 0, False
    for seg in segments:
        if seg in narrow:
            if not narrow_done:
                o_ref[:, off:off + LANES] = narrow_tile
                off += LANES
                narrow_done = True
            continue
        width = seg[1] - seg[0]
        for c0 in range(0, width, FILL_COLS):
            c1 = min(c0 + FILL_COLS, width)
            o_ref[:, off + c0:off + c1] = wt_ref[seg[0] + c0:seg[0] + c1, :].T.astype(o_ref.dtype)
        off += width
    assert off == W_COLS


def _relayout_projection(w_in, segments):
    depth, d, w = w_in.shape
    rows = RELAYOUT_ROWS
    assert d % rows == 0
    return pl.pallas_call(
        functools.partial(_relayout_kernel, segments=segments),
        grid=(depth, d // rows),
        in_specs=[pl.BlockSpec((None, w, rows), lambda l, r: (l, 0, r))],
        out_specs=pl.BlockSpec((None, rows, W_COLS), lambda l, r: (l, r, 0)),
        out_shape=jax.ShapeDtypeStruct((depth, d, W_COLS), MXU_DTYPE),
        compiler_params=pltpu.CompilerParams(dimension_semantics=("arbitrary", "arbitrary"),
                                             vmem_limit_bytes=VMEM_LIMIT_BYTES),
        name="relayout_projection",
    )(jnp.swapaxes(w_in, 1, 2))


def _prepare_operands(norm_g, w_in, a_conv_w, a_conv_b, a_gate_a_w, a_gate_a_b, a_gate_x_w, a_gate_x_b,
                      a_lambda, b_conv_w, b_conv_b, b_dt_bias, b_a_log, b_d_skip, b_norm_g, c_i_bias,
                      c_f_bias, c_norm_g, d_norm_g, w_branch, w_out, final_norm_g):
    depth = w_in.shape[0]
    o = 0
    a0 = o; o += 2 * BR
    b0 = o; o += 3 * BR
    dt0 = o; o += SSD_HEADS
    c0 = o; o += 5 * BR
    if0 = o; o += 2 * ML_HEADS
    d0 = o; o += 2 * RET_HEADS * RET_QK + 2 * BR
    g0 = o; o += 4 * D_MODEL
    assert o == w_in.shape[2]
    segments = ((a0, b0), (dt0, c0), (if0, d0), None, (b0, dt0), (c0, if0), (d0, o))
    w_all = _relayout_projection(w_in, segments)

    def gate_blocks(t):
        per_half = RG_BLOCKS // 2
        blk = t.shape[-1]
        rows = [jnp.pad(t[:, h], ((0, 0), (0, 0), ((h % per_half) * blk, (per_half - 1 - h % per_half) * blk)))
                for h in range(RG_BLOCKS)]
        halves = [jnp.concatenate(rows[j * per_half:(j + 1) * per_half], axis=1) for j in range(2)]
        return jnp.stack(halves, axis=1)

    agate = jnp.concatenate([gate_blocks(a_gate_a_w), gate_blocks(a_gate_x_w)], axis=3).astype(MXU_DTYPE)

    def row(v, width=D_MODEL):
        return jnp.pad(v.astype(F32), ((0, 0), (0, width - v.shape[1])))[:, None, :]

    def rows(v):
        return jnp.pad(v.astype(F32), ((0, 0), (0, 0), (0, D_MODEL - v.shape[2])))

    table = {
        R_NORM_G: row(norm_g),
        R_FINAL_G: row(jnp.broadcast_to(final_norm_g, (depth, D_MODEL))),
        R_A_CONVW: rows(a_conv_w),
        R_A_CONVB: row(a_conv_b),
        R_A_GATEB: row(jnp.concatenate([a_gate_a_b, a_gate_x_b], axis=1)),
        R_A_LAM: row(a_lambda),
        R_B_CONVW: rows(b_conv_w),
        R_B_CONVB: row(b_conv_b),
        R_B_DSKIP: row(jnp.repeat(b_d_skip, SSD_HEAD_DIM, axis=1)),
        R_B_NORMG: row(b_norm_g),
        R_C_NORMG: row(c_norm_g),
        R_D_NORMG: row(d_norm_g),
        R_B_DTBIAS: row(b_dt_bias),
        R_B_ALOG: row(b_a_log),
        R_C_IBIAS: row(c_i_bias),
        R_C_FBIAS: row(c_f_bias),
    }
    pieces, r = [], 0
    for start in sorted(table):
        assert start == r
        pieces.append(table[start])
        r += table[start].shape[1]
    pieces.append(jnp.zeros((depth, VEC_ROWS - r, D_MODEL), F32))
    vecs = jnp.concatenate(pieces, axis=1)
    return vecs, w_all, agate, w_branch.astype(MXU_DTYPE), w_out.astype(MXU_DTYPE)


def kernel(x, norm_g, w_in, a_conv_w, a_conv_b, a_gate_a_w, a_gate_a_b, a_gate_x_w, a_gate_x_b, a_lambda,
           b_conv_w, b_conv_b, b_dt_bias, b_a_log, b_d_skip, b_norm_g, c_i_bias, c_f_bias, c_norm_g,
           d_norm_g, w_branch, w_out, final_norm_g):
    depth = w_in.shape[0]
    seq = x.shape[1]
    ts = min(SEQ_TILE, seq)
    lc = min(CHUNK, ts // 2)
    rope_tab, dtab, dmat = _retention_tables(seq, lc)
    ops = _prepare_operands(norm_g, w_in, a_conv_w, a_conv_b, a_gate_a_w, a_gate_a_b, a_gate_x_w, a_gate_x_b,
                            a_lambda, b_conv_w, b_conv_b, b_dt_bias, b_a_log, b_d_skip, b_norm_g, c_i_bias,
                            c_f_bias, c_norm_g, d_norm_g, w_branch, w_out, final_norm_g)
    for l in range(depth):
        x = _layer_call(x, l, *ops, rope_tab, dtab, dmat, ts=ts, lc=lc,
                        first_layer=(l == 0), last_layer=(l == depth - 1))
    return x
```

```python
import functools

import jax
import jax.numpy as jnp
from jax import lax
from jax.experimental import pallas as pl
from jax.experimental.pallas import tpu as pltpu

F32 = jnp.float32
MXU_DTYPE = jnp.bfloat16

D_MODEL = 1024
BR = 512
EPS = 1e-6
CONV_W = 4
RG_BLOCKS = 8
RG_C = 8.0
SSD_HEADS = 8
SSD_GROUPS = 2
SSD_HPG = 4
SSD_HEAD_DIM = 64
SSD_STATE = 128
ML_HEADS = 4
ML_HEAD_DIM = 128
RET_HEADS = 4
RET_QK = 64
RET_V = 128
ROPE_BASE = 10000.0
LANES = 128
SUBLANES = 8

SEQ_TILE = 256
CHUNK = 128
RELAYOUT_ROWS = 128
FILL_AFTER_GATES = 9
FILL_COLS = 512
VMEM_LIMIT_BYTES = 60 * 1024 * 1024

(R_NORM_G, R_FINAL_G, R_A_CONVW, R_A_CONVB, R_A_GATEB, R_A_LAM, R_B_CONVW, R_B_CONVB, R_B_DSKIP,
 R_B_NORMG, R_C_NORMG, R_D_NORMG, R_B_DTBIAS, R_B_ALOG, R_C_IBIAS, R_C_FBIAS) = (
    0, 1, 2, 6, 7, 8, 9, 13, 14, 15, 16, 17, 18, 19, 20, 21)
VEC_ROWS = 24

SMALL_DT, SMALL_I, SMALL_F = 0, 8, 12

OFF_A = 0
OFF_S = OFF_A + 2 * BR
OFF_B = OFF_S + LANES
OFF_C = OFF_B + 3 * BR
OFF_D = OFF_C + 5 * BR
OFF_G = OFF_D + 2 * RET_HEADS * RET_QK + 2 * BR
W_COLS = OFF_G + 4 * D_MODEL


def _mm(a, b):
    return jnp.dot(a.astype(MXU_DTYPE), b.astype(MXU_DTYPE), preferred_element_type=F32)


def _mm_nt(a, b):
    return lax.dot_general(a.astype(MXU_DTYPE), b.astype(MXU_DTYPE), (((1,), (1,)), ((), ())),
                           preferred_element_type=F32)


def _mm_tn(a, b):
    return lax.dot_general(a.astype(MXU_DTYPE), b.astype(MXU_DTYPE), (((0,), (0,)), ((), ())),
                           preferred_element_type=F32)


def _sigmoid(x):
    return 0.5 * jnp.tanh(0.5 * x) + 0.5


def _silu(x):
    return x * _sigmoid(x)


def _softplus(x):
    return jnp.maximum(x, 0.0) + jnp.log1p(jnp.exp(-jnp.abs(x)))


def _rmsnorm(x, g):
    return x * lax.rsqrt(jnp.mean(x * x, axis=-1, keepdims=True) + EPS) * g


def _headwise_rmsnorm(y, g, n_heads):
    w = y.shape[1] // n_heads
    parts = []
    for h in range(n_heads):
        yh = y[:, h * w:(h + 1) * w]
        parts.append(yh * lax.rsqrt(jnp.mean(yh * yh, axis=-1, keepdims=True) + EPS))
    return jnp.concatenate(parts, axis=1) * g


def _to_strided(x, lc):
    nv = lc // SUBLANES
    return jnp.concatenate(
        [jnp.swapaxes(x[c * lc:(c + 1) * lc, :].reshape(SUBLANES, nv, x.shape[1]), 0, 1).reshape(lc, x.shape[1])
         for c in range(x.shape[0] // lc)], axis=0)


def _from_strided(x, lc):
    nv = lc // SUBLANES
    return jnp.concatenate(
        [jnp.swapaxes(x[c * lc:(c + 1) * lc, :].reshape(nv, SUBLANES, x.shape[1]), 0, 1).reshape(lc, x.shape[1])
         for c in range(x.shape[0] // lc)], axis=0)


def _strided_time(shape, axis, lc):
    i = lax.broadcasted_iota(jnp.int32, shape, axis)
    return (i & (SUBLANES - 1)) * (lc // SUBLANES) + (i >> 3)


def _sublane(width):
    return lax.broadcasted_iota(jnp.int32, (SUBLANES, width), 0)


def _slabs(v, r0, n):
    return [v[r0 + i * SUBLANES:r0 + (i + 1) * SUBLANES, :] for i in range(n)]


def _cumsum_time(v, lc):
    nv = lc // SUBLANES
    sub = _sublane(v.shape[1])
    out = []
    for c in range(v.shape[0] // lc):
        acc = _slabs(v, c * lc, nv)
        for i in range(1, nv):
            acc[i] = acc[i - 1] + acc[i]
        tot = acc[-1]
        inc = tot
        for k in (1, 2, 4):
            inc = inc + jnp.where(sub >= k, pltpu.roll(inc, k, 0), 0.0)
        out += [a + (inc - tot) for a in acc]
    return jnp.concatenate(out, axis=0)


def _scan_time(a, d, h_prev, lc):
    nv = lc // SUBLANES
    sub = _sublane(a.shape[1])
    out = []
    for c in range(a.shape[0] // lc):
        p, h = _slabs(a, c * lc, nv), _slabs(d, c * lc, nv)
        for i in range(1, nv):
            h[i] = p[i] * h[i - 1] + h[i]
            p[i] = p[i] * p[i - 1]
        pt, ht = p[-1], h[-1]
        for k in (1, 2, 4):
            keep = sub >= k
            ht = pt * jnp.where(keep, pltpu.roll(ht, k, 0), 0.0) + ht
            pt = pt * jnp.where(keep, pltpu.roll(pt, k, 0), 1.0)
        ends = ht + pt * h_prev
        carry = jnp.where(sub == 0, h_prev, pltpu.roll(ends, 1, 0))
        out += [h[i] + p[i] * carry for i in range(nv)]
        h_prev = ends[SUBLANES - 1:SUBLANES, :]
    return jnp.concatenate(out, axis=0), h_prev


def _chunk_last_bcast(v, lc):
    n = v.shape[0] // lc
    return jnp.concatenate(
        [jnp.broadcast_to(v[(c + 1) * lc - 1:(c + 1) * lc, :], (lc, v.shape[1])) for c in range(n)], axis=0)


def _head_bcast(v, n_heads, head_w):
    r = v.shape[0]
    cols = [jnp.broadcast_to(v[:, h:h + 1], (r, LANES)) for h in range(n_heads)]
    if head_w == LANES:
        return jnp.concatenate(cols, axis=1)
    assert head_w * 2 == LANES
    lane = lax.broadcasted_iota(jnp.int32, (r, LANES), 1)
    return jnp.concatenate(
        [jnp.where(lane < head_w, cols[2 * j], cols[2 * j + 1]) for j in range(n_heads // 2)], axis=1)


def _block_rows(x, n_blocks):
    rows, w = x.shape
    bl = w // n_blocks
    n_tiles = w // LANES
    zero = jnp.zeros((rows, LANES), MXU_DTYPE)
    half = lax.broadcasted_iota(jnp.int32, (rows, LANES), 1) // bl if bl < LANES else None
    out = []
    for h in range(n_blocks):
        tiles = []
        for j in range(n_tiles):
            if bl >= LANES:
                keep = (j * LANES) // bl == h
                tiles.append(x[:, j * LANES:(j + 1) * LANES].astype(MXU_DTYPE) if keep else zero)
            elif j == (h * bl) // LANES:
                tiles.append(jnp.where(half == h % (LANES // bl), x[:, j * LANES:(j + 1) * LANES], 0.0)
                             .astype(MXU_DTYPE))
            else:
                tiles.append(zero)
        out.append(jnp.concatenate(tiles, axis=1))
    return jnp.concatenate(out, axis=0)


def _block_diag(blocks):
    n = len(blocks)
    zero = jnp.zeros(blocks[0].shape, MXU_DTYPE)
    return jnp.concatenate(
        [jnp.concatenate([blocks[i].astype(MXU_DTYPE) if j == i else zero for j in range(n)], axis=1)
         for i in range(n)], axis=0)


def _lane_bcast_heads(cols, width):
    return jnp.concatenate([jnp.broadcast_to(c, (c.shape[0], width)) for c in cols], axis=1)


def _causal_conv(tail_ref, x, w, b, lc):
    nt = (CONV_W - 1) * SUBLANES
    sub0 = _sublane(x.shape[1]) == 0
    prev_tail = tail_ref[...]
    out = []
    for c in range(x.shape[0] // lc):
        xc = x[c * lc:(c + 1) * lc, :]
        cur_tail = xc[lc - nt:lc, :]
        wrap = jnp.concatenate(
            [jnp.where(sub0, pltpu.roll(p, 1, 0), pltpu.roll(q, 1, 0))
             for p, q in zip(_slabs(prev_tail, 0, CONV_W - 1), _slabs(cur_tail, 0, CONV_W - 1))], axis=0)
        acc = b + w[CONV_W - 1:CONV_W, :] * xc
        for sh in range(1, CONV_W):
            shifted = jnp.concatenate([wrap[nt - sh * SUBLANES:nt, :], xc[0:lc - sh * SUBLANES, :]], axis=0)
            acc = acc + w[CONV_W - 1 - sh:CONV_W - sh, :] * shifted
        out.append(acc)
        prev_tail = cur_tail
    tail_ref[...] = prev_tail
    return jnp.concatenate(out, axis=0)


def _layer_kernel(x_ref, vec_ref, w_ref, agate_ref, wbr_ref, wout_ref, rope_ref, dtab_ref, dmat_ref, o_ref,
                  a_tail, a_h, b_tail, b_state, c_state, c_n, c_m, d_state,
                  *, ts, lc, first_layer, last_layer):
    nchunk = ts // lc

    @pl.when(pl.program_id(1) == 0)
    def _():
        a_tail[...] = jnp.zeros_like(a_tail)
        a_h[...] = jnp.zeros_like(a_h)
        b_tail[...] = jnp.zeros_like(b_tail)
        b_state[...] = jnp.zeros_like(b_state)
        c_state[...] = jnp.zeros_like(c_state)
        c_n[...] = jnp.zeros_like(c_n)
        c_m[...] = jnp.zeros_like(c_m)
        d_state[...] = jnp.zeros_like(d_state)

    def vec(row, width, nrows=1):
        return vec_ref[row:row + nrows, 0:width]

    half = ts // 2
    x_halves, hn_halves, pa_halves = [], [], []
    for r0 in (0, half):
        xh_ = x_ref[0, r0:r0 + half, :]
        xh_ = _to_strided(xh_, lc) if first_layer else xh_
        hh_ = _rmsnorm(xh_, vec(R_NORM_G, D_MODEL)).astype(MXU_DTYPE)
        x_halves.append(xh_)
        hn_halves.append(hh_)
        pa_halves.append(_mm(hh_, w_ref[:, OFF_A:OFF_B]))
    hn = jnp.concatenate(hn_halves, axis=0)
    pa = jnp.concatenate(pa_halves, axis=0)
    small = pa[:, 2 * BR:2 * BR + LANES]

    pending, pieces = [], {}

    def defer_projection(key, c0, c1):
        pieces[key] = []
        for a in range(c0, c1, FILL_COLS):
            pending.append((key, a, min(a + FILL_COLS, c1)))

    def fill(n=1):
        for _ in range(min(n, len(pending))):
            key, a, b = pending.pop(0)
            pieces[key].append(_mm(hn, w_ref[:, a:b]))

    def take(key):
        while any(k == key for k, _, _ in pending):
            fill()
        return jnp.concatenate(pieces.pop(key), axis=1)

    defer_projection("b", OFF_B, OFF_C)
    defer_projection("c", OFF_C, OFF_D)
    defer_projection("d", OFF_D, OFF_G)
    for n in range(4):
        defer_projection(f"g{n}", OFF_G + n * D_MODEL, OFF_G + (n + 1) * D_MODEL)
    branch_out = {}

    def gated_up(n, y):
        return _sigmoid(take(f"g{n}")) * _mm(y, wbr_ref[n])

    causal = _strided_time((lc, lc), 0, lc) >= _strided_time((lc, lc), 1, lc)
    neg_inf = jnp.float32(-jnp.inf)

    u, za = pa[:, 0:BR], pa[:, BR:2 * BR]
    xc = _causal_conv(a_tail, u, vec(R_A_CONVW, BR, CONV_W), vec(R_A_CONVB, BR), lc)
    hw = BR // 2
    fill(2)
    g_halves = [_mm(xc[:, j * hw:(j + 1) * hw], agate_ref[j]) for j in range(2)]
    fill(FILL_AFTER_GATES)
    gate_b = vec(R_A_GATEB, 2 * BR)
    r_gate = _sigmoid(jnp.concatenate([g[:, 0:hw] for g in g_halves], axis=1) + gate_b[:, 0:BR])
    i_gate = _sigmoid(jnp.concatenate([g[:, hw:2 * hw] for g in g_halves], axis=1) + gate_b[:, BR:2 * BR])
    log_a = (-RG_C) * r_gate * _softplus(-vec(R_A_LAM, BR))
    a = jnp.exp(log_a)
    drive = jnp.sqrt(1.0 - a * a) * (i_gate * xc)
    h, h_last = _scan_time(a, drive, a_h[0:1, :], lc)
    a_h[0:1, :] = h_last
    merged = gated_up(0, h * _silu(za))

    def ssd_mixer():
        pb = take("b")
        zb = pb[:, 2 * BR:3 * BR]
        xbc = _silu(_causal_conv(b_tail, pb[:, 0:2 * BR], vec(R_B_CONVW, 2 * BR, CONV_W),
                                 vec(R_B_CONVB, 2 * BR), lc))
        xh = xbc[:, 0:BR]
        bm = xbc[:, BR:BR + 2 * SSD_STATE]
        cm = xbc[:, BR + 2 * SSD_STATE:BR + 4 * SSD_STATE]
        dt = _softplus(small + vec(R_B_DTBIAS, LANES))
        a_dt = -jnp.exp(vec(R_B_ALOG, LANES)) * dt
        a_cs = _cumsum_time(a_dt, lc)
        a_tot = _chunk_last_bcast(a_cs, lc)
        dt_f = _head_bcast(dt, SSD_HEADS, SSD_HEAD_DIM)
        eacs_f = _head_bcast(jnp.exp(a_cs), SSD_HEADS, SSD_HEAD_DIM)
        dte_f = _head_bcast(jnp.exp(a_tot - a_cs), SSD_HEADS, SSD_HEAD_DIM)
        x_dt = xh * dt_f
        xs = x_dt * dte_f
        gw = SSD_HPG * SSD_HEAD_DIM
        y_rows = []
        for c in range(nchunk):
            r0, r1 = c * lc, (c + 1) * lc
            yield
            acs_c = a_cs[r0:r1, :]
            acs_t = acs_c.T
            c_c, b_c = cm[r0:r1, :], bm[r0:r1, :]
            cb = _mm_nt(c_c, _block_rows(b_c, SSD_GROUPS))
            yield
            y_groups = []
            for g in range(SSD_GROUPS):
                cb_g = cb[:, g * lc:(g + 1) * lc]
                masked = []
                for e in range(SSD_HPG):
                    hh = g * SSD_HPG + e
                    diff = acs_c[:, hh:hh + 1] - acs_t[hh:hh + 1, :]
                    decay = jnp.exp(jnp.where(causal, diff, neg_inf))
                    masked.append((cb_g * decay).astype(MXU_DTYPE))
                y_groups.append(_mm(jnp.concatenate(masked, axis=1),
                                    _block_rows(x_dt[r0:r1, g * gw:(g + 1) * gw], SSD_HPG)))
            states = [b_state[g] for g in range(SSD_GROUPS)]
            y_off = _mm(c_c, _block_diag(states)) * eacs_f[r0:r1, :]
            y_rows.append(jnp.concatenate(y_groups, axis=1) + y_off)
            for g in range(SSD_GROUPS):
                b_state[g] = (eacs_f[r1 - 1:r1, g * gw:(g + 1) * gw] * states[g]
                              + _mm_tn(b_c[:, g * SSD_STATE:(g + 1) * SSD_STATE], xs[r0:r1, g * gw:(g + 1) * gw]))
        y_b = jnp.concatenate(y_rows, axis=0) + vec(R_B_DSKIP, BR) * xh
        branch_out[1] = _rmsnorm(y_b * _silu(zb), vec(R_B_NORMG, BR))

    def mlstm_mixer():
        pc = take("c")
        qc = pc[:, 0:BR]
        kc = pc[:, BR:2 * BR] * (ML_HEAD_DIM ** -0.5)
        vc = pc[:, 2 * BR:3 * BR]
        oc = pc[:, 3 * BR:4 * BR]
        zc = pc[:, 4 * BR:5 * BR]
        log_i = pltpu.roll(small, LANES - SMALL_I, 1) + vec(R_C_IBIAS, LANES)
        log_f = -_softplus(-(pltpu.roll(small, LANES - SMALL_F, 1) + vec(R_C_FBIAS, LANES)))
        f_cs = _cumsum_time(log_f, lc)
        f_tot = _chunk_last_bcast(f_cs, lc)
        w_end = f_tot - f_cs + log_i
        g_src = log_i - f_cs
        m_state = c_m[0:1, :]
        h_rows = []
        for c in range(nchunk):
            r0, r1 = c * lc, (c + 1) * lc
            yield
            fcs_c = f_cs[r0:r1, :]
            w_c = w_end[r0:r1, :]
            m_loc = jnp.max(w_c, axis=0, keepdims=True)
            p_end = jnp.exp(w_c - m_loc)
            g_t = g_src[r0:r1, :].T
            ft = f_tot[r1 - 1:r1, :]
            m_new = jnp.maximum(ft + m_state, m_loc)
            s_old = jnp.exp(ft + m_state - m_new)
            s_loc = jnp.exp(m_loc - m_new)
            q_c, k_c, v_c = qc[r0:r1, :], kc[r0:r1, :], vc[r0:r1, :]
            pairs = [slice(p * 2 * ML_HEAD_DIM, (p + 1) * 2 * ML_HEAD_DIM) for p in range(ML_HEADS // 2)]
            qk = jnp.concatenate([_mm_nt(q_c[:, pr], _block_rows(k_c[:, pr], 2)) for pr in pairs],
                                 axis=1)
            yield
            n_prev = c_n[0:1, :]
            qn = q_c * n_prev
            scores, scale_prev, inv = [], [], []
            for hh in range(ML_HEADS):
                fcol = fcs_c[:, hh:hh + 1]
                log_d = jnp.where(causal, fcol + g_t[hh:hh + 1, :], neg_inf)
                log_prev = fcol + m_state[:, hh:hh + 1]
                m_row = jnp.maximum(log_prev, jnp.max(log_d, axis=1, keepdims=True))
                s_h = qk[:, hh * lc:(hh + 1) * lc] * jnp.exp(log_d - m_row)
                sp = jnp.exp(log_prev - m_row)
                den = (jnp.sum(s_h, axis=1, keepdims=True)
                       + sp * jnp.sum(qn[:, hh * ML_HEAD_DIM:(hh + 1) * ML_HEAD_DIM], axis=1, keepdims=True))
                scores.append(s_h.astype(MXU_DTYPE))
                scale_prev.append(sp)
                inv.append(1.0 / jnp.maximum(jnp.abs(den), jnp.exp(-m_row)))
            c_prev = [c_state[hh] for hh in range(ML_HEADS)]
            intra = jnp.concatenate(
                [_mm(jnp.concatenate(scores[2 * p:2 * p + 2], axis=1), _block_rows(v_c[:, pr], 2))
                 for p, pr in enumerate(pairs)], axis=1)
            inter = jnp.concatenate(
                [_mm_nt(q_c[:, pr], _block_diag(c_prev[2 * p:2 * p + 2])) for p, pr in enumerate(pairs)], axis=1)
            num = intra + _lane_bcast_heads(scale_prev, ML_HEAD_DIM) * inter
            h_rows.append(num * _lane_bcast_heads(inv, ML_HEAD_DIM))
            p_f = _lane_bcast_heads([p_end[:, hh:hh + 1] for hh in range(ML_HEADS)], ML_HEAD_DIM)
            pv = p_f * v_c
            for p, pr in enumerate(pairs):
                c_loc = _mm_tn(pv[:, pr], k_c[:, pr])
                for e in range(2):
                    hh = 2 * p + e
                    d0, d1 = e * ML_HEAD_DIM, (e + 1) * ML_HEAD_DIM
                    c_state[hh] = s_old[:, hh:hh + 1] * c_prev[hh] + s_loc[:, hh:hh + 1] * c_loc[d0:d1, d0:d1]
            so_f = _lane_bcast_heads([s_old[:, hh:hh + 1] for hh in range(ML_HEADS)], ML_HEAD_DIM)
            sl_f = _lane_bcast_heads([s_loc[:, hh:hh + 1] for hh in range(ML_HEADS)], ML_HEAD_DIM)
            c_n[0:1, :] = so_f * n_prev + sl_f * jnp.sum(p_f * k_c, axis=0, keepdims=True)
            m_state = m_new
        c_m[0:1, :] = m_state
        y_c = _sigmoid(oc) * jnp.concatenate(h_rows, axis=0)
        branch_out[2] = _headwise_rmsnorm(y_c, vec(R_C_NORMG, BR), ML_HEADS) * _silu(zc)

    def retention_mixer():
        pd = take("d")
        qk_w = RET_HEADS * RET_QK
        vd = pd[:, 2 * qk_w:2 * qk_w + BR]
        zd = pd[:, 2 * qk_w + BR:2 * qk_w + 2 * BR]
        cos2 = jnp.concatenate([rope_ref[:, 0:LANES]] * 2, axis=1)
        sin2 = jnp.concatenate([rope_ref[:, LANES:2 * LANES]] * 2, axis=1)
        lane_q = lax.broadcasted_iota(jnp.int32, (ts, qk_w), 1)
        first_half = (lane_q & (RET_QK - 1)) < RET_QK // 2

        def rope(t):
            partner = jnp.where(first_half, pltpu.roll(t, qk_w - RET_QK // 2, 1), pltpu.roll(t, RET_QK // 2, 1))
            return t * cos2 + partner * sin2

        qr = rope(pd[:, 0:qk_w])
        kr = rope(pd[:, qk_w:2 * qk_w]) * (RET_QK ** -0.5)
        dec_start = dtab_ref[0:lc, 0:qk_w]
        dec_end = dtab_ref[0:lc, qk_w:2 * qk_w]
        y_rows = []
        for c in range(nchunk):
            r0, r1 = c * lc, (c + 1) * lc
            yield
            qr_c, kr_c, v_c = qr[r0:r1, :], kr[r0:r1, :], vd[r0:r1, :]
            inner = _mm_nt(qr_c, _block_rows(kr_c, RET_HEADS)) * dmat_ref[...]
            yield
            r_prev = [d_state[hh] for hh in range(RET_HEADS)]
            intra = jnp.concatenate(
                [_mm(inner[:, p * 2 * lc:(p + 1) * 2 * lc], _block_rows(v_c[:, p * 2 * RET_V:(p + 1) * 2 * RET_V], 2))
                 for p in range(RET_HEADS // 2)], axis=1)
            y_rows.append(intra + _mm(qr_c * dec_start, _block_diag(r_prev)))
            kd_c = kr_c * dec_end
            for p in range(RET_HEADS // 2):
                upd = _mm_tn(kd_c[:, p * 2 * RET_QK:(p + 1) * 2 * RET_QK],
                             v_c[:, p * 2 * RET_V:(p + 1) * 2 * RET_V])
                for e in range(2):
                    hh = 2 * p + e
                    chunk_dec = dtab_ref[lc + hh:lc + hh + 1, 0:RET_V]
                    d_state[hh] = (chunk_dec * r_prev[hh]
                                   + upd[e * RET_QK:(e + 1) * RET_QK, e * RET_V:(e + 1) * RET_V])
        y_d = jnp.concatenate(y_rows, axis=0)
        branch_out[3] = _headwise_rmsnorm(y_d, vec(R_D_NORMG, BR), RET_HEADS) * _silu(zd)

    mixers = [ssd_mixer(), mlstm_mixer(), retention_mixer()]
    while mixers:
        for m in list(mixers):
            if next(m, "done") == "done":
                mixers.remove(m)
        fill()
    for n in (1, 2, 3):
        merged = merged + gated_up(n, branch_out[n])
    for r0, x_half in zip((0, half), x_halves):
        out = x_half + _mm(merged[r0:r0 + half, :], wout_ref[...])
        if last_layer:
            out = _from_strided(_rmsnorm(out, vec(R_FINAL_G, D_MODEL)), lc)
        o_ref[0, r0:r0 + half, :] = out


def _const_spec(shape):
    zeros = (0,) * len(shape)
    return pl.BlockSpec(shape, lambda b, s: zeros, pipeline_mode=pl.Buffered(1))


def _layer_spec(arr, layer):
    idx = (layer,) + (0,) * (arr.ndim - 1)
    return pl.BlockSpec((None,) + arr.shape[1:], lambda b, s: idx, pipeline_mode=pl.Buffered(1))


def _layer_call(x, layer, vecs, w_all, agate, wbr, wout, rope_tab, dtab, dmat,
                *, ts, lc, first_layer, last_layer):
    bsz, seq, d = x.shape
    assert d == D_MODEL and seq % ts == 0 and ts % (2 * lc) == 0 and lc % LANES == 0
    kern = functools.partial(_layer_kernel, ts=ts, lc=lc, first_layer=first_layer, last_layer=last_layer)
    consts = (vecs, w_all, agate, wbr, wout)
    in_specs = ([pl.BlockSpec((1, ts, d), lambda b, s: (b, s, 0))]
                + [_layer_spec(c, layer) for c in consts]
                + [pl.BlockSpec((ts, 2 * LANES), lambda b, s: (s, 0)),
                   _const_spec(dtab.shape), _const_spec(dmat.shape)])
    scratch = [
        pltpu.VMEM(((CONV_W - 1) * SUBLANES, BR), F32),
        pltpu.VMEM((SUBLANES, BR), F32),
        pltpu.VMEM(((CONV_W - 1) * SUBLANES, 2 * BR), F32),
        pltpu.VMEM((SSD_GROUPS, SSD_STATE, SSD_HPG * SSD_HEAD_DIM), F32),
        pltpu.VMEM((ML_HEADS, ML_HEAD_DIM, ML_HEAD_DIM), F32),
        pltpu.VMEM((SUBLANES, BR), F32),
        pltpu.VMEM((SUBLANES, LANES), F32),
        pltpu.VMEM((RET_HEADS, RET_QK, RET_V), F32),
    ]
    return pl.pallas_call(
        kern,
        grid=(bsz, seq // ts),
        in_specs=in_specs,
        out_specs=pl.BlockSpec((1, ts, d), lambda b, s: (b, s, 0)),
        out_shape=jax.ShapeDtypeStruct(x.shape, x.dtype),
        scratch_shapes=scratch,
        compiler_params=pltpu.CompilerParams(
            dimension_semantics=("arbitrary", "arbitrary"),
            vmem_limit_bytes=VMEM_LIMIT_BYTES),
        name="hybrid_layer",
    )(x, *consts, rope_tab, dtab, dmat)


def _retention_tables(seq, lc):
    half = RET_QK // 2
    nv = lc // SUBLANES
    inv = ROPE_BASE ** (-jnp.arange(half, dtype=F32) / half)
    row = jnp.arange(seq)
    pos = (row // lc) * lc + (row % SUBLANES) * nv + (row % lc) // SUBLANES
    ang = pos.astype(F32)[:, None] * inv
    cos, sin = jnp.cos(ang), jnp.sin(ang)
    rope_tab = jnp.concatenate([cos] * (LANES // half) + [-sin, sin] * (LANES // RET_QK), axis=1)
    log_g = jnp.log1p(-jnp.exp2(-5.0 - jnp.arange(RET_HEADS, dtype=F32)))
    idx = pos[:lc].astype(F32)
    rel = idx[:, None] - idx[None, :]
    dmat = jnp.where(rel >= 0, jnp.exp(log_g[:, None, None] * jnp.maximum(rel, 0.0)), 0.0)
    dmat = dmat.transpose(1, 0, 2).reshape(lc, RET_HEADS * lc)
    dec_start = jnp.repeat(jnp.exp(log_g[None, :] * (idx[:, None] + 1.0)), RET_QK, axis=1)
    dec_end = jnp.repeat(jnp.exp(log_g[None, :] * (lc - 1.0 - idx[:, None])), RET_QK, axis=1)
    chunk_dec = jnp.broadcast_to(jnp.exp(log_g * lc)[:, None], (RET_HEADS, 2 * RET_HEADS * RET_QK))
    dtab = jnp.concatenate([jnp.concatenate([dec_start, dec_end], axis=1), chunk_dec,
                            jnp.zeros((SUBLANES - RET_HEADS, 2 * RET_HEADS * RET_QK), F32)], axis=0)
    return rope_tab, dtab, dmat


def _relayout_kernel(wt_ref, o_ref, *, segments):
    narrow = [s for s in segments if s is None or s[1] - s[0] < LANES]
    parts = [wt_ref[s[0]:s[1], :] for s in narrow if s is not None]
    used = sum(p.shape[0] for p in parts)
    parts.append(jnp.zeros((LANES - used, wt_ref.shape[1]), F32))
    narrow_tile = jnp.concatenate(parts, axis=0).T.astype(o_ref.dtype)
    off, narrow_done = 0, False
    for seg in segments:
        if seg in narrow:
            if not narrow_done:
                o_ref[:, off:off + LANES] = narrow_tile
                off += LANES
                narrow_done = True
            continue
        width = seg[1] - seg[0]
        for c0 in range(0, width, FILL_COLS):
            c1 = min(c0 + FILL_COLS, width)
            o_ref[:, off + c0:off + c1] = wt_ref[seg[0] + c0:seg[0] + c1, :].T.astype(o_ref.dtype)
        off += width
    assert off == W_COLS


def _relayout_projection(w_in, segments):
    depth, d, w = w_in.shape
    rows = RELAYOUT_ROWS
    assert d % rows == 0
    return pl.pallas_call(
        functools.partial(_relayout_kernel, segments=segments),
        grid=(depth, d // rows),
        in_specs=[pl.BlockSpec((None, w, rows), lambda l, r: (l, 0, r))],
        out_specs=pl.BlockSpec((None, rows, W_COLS), lambda l, r: (l, r, 0)),
        out_shape=jax.ShapeDtypeStruct((depth, d, W_COLS), MXU_DTYPE),
        compiler_params=pltpu.CompilerParams(dimension_semantics=("arbitrary", "arbitrary"),
                                             vmem_limit_bytes=VMEM_LIMIT_BYTES),
        name="relayout_projection",
    )(jnp.swapaxes(w_in, 1, 2))


def _prepare_operands(norm_g, w_in, a_conv_w, a_conv_b, a_gate_a_w, a_gate_a_b, a_gate_x_w, a_gate_x_b,
                      a_lambda, b_conv_w, b_conv_b, b_dt_bias, b_a_log, b_d_skip, b_norm_g, c_i_bias,
                      c_f_bias, c_norm_g, d_norm_g, w_branch, w_out, final_norm_g):
    depth = w_in.shape[0]
    o = 0
    a0 = o; o += 2 * BR
    b0 = o; o += 3 * BR
    dt0 = o; o += SSD_HEADS
    c0 = o; o += 5 * BR
    if0 = o; o += 2 * ML_HEADS
    d0 = o; o += 2 * RET_HEADS * RET_QK + 2 * BR
    g0 = o; o += 4 * D_MODEL
    assert o == w_in.shape[2]
    segments = ((a0, b0), (dt0, c0), (if0, d0), None, (b0, dt0), (c0, if0), (d0, o))
    w_all = _relayout_projection(w_in, segments)

    def gate_blocks(t):
        per_half = RG_BLOCKS // 2
        blk = t.shape[-1]
        rows = [jnp.pad(t[:, h], ((0, 0), (0, 0), ((h % per_half) * blk, (per_half - 1 - h % per_half) * blk)))
                for h in range(RG_BLOCKS)]
        halves = [jnp.concatenate(rows[j * per_half:(j + 1) * per_half], axis=1) for j in range(2)]
        return jnp.stack(halves, axis=1)

    agate = jnp.concatenate([gate_blocks(a_gate_a_w), gate_blocks(a_gate_x_w)], axis=3).astype(MXU_DTYPE)

    def row(v, width=D_MODEL):
        return jnp.pad(v.astype(F32), ((0, 0), (0, width - v.shape[1])))[:, None, :]

    def rows(v):
        return jnp.pad(v.astype(F32), ((0, 0), (0, 0), (0, D_MODEL - v.shape[2])))

    table = {
        R_NORM_G: row(norm_g),
        R_FINAL_G: row(jnp.broadcast_to(final_norm_g, (depth, D_MODEL))),
        R_A_CONVW: rows(a_conv_w),
        R_A_CONVB: row(a_conv_b),
        R_A_GATEB: row(jnp.concatenate([a_gate_a_b, a_gate_x_b], axis=1)),
        R_A_LAM: row(a_lambda),
        R_B_CONVW: rows(b_conv_w),
        R_B_CONVB: row(b_conv_b),
        R_B_DSKIP: row(jnp.repeat(b_d_skip, SSD_HEAD_DIM, axis=1)),
        R_B_NORMG: row(b_norm_g),
        R_C_NORMG: row(c_norm_g),
        R_D_NORMG: row(d_norm_g),
        R_B_DTBIAS: row(b_dt_bias),
        R_B_ALOG: row(b_a_log),
        R_C_IBIAS: row(c_i_bias),
        R_C_FBIAS: row(c_f_bias),
    }
    pieces, r = [], 0
    for start in sorted(table):
        assert start == r
        pieces.append(table[start])
        r += table[start].shape[1]
    pieces.append(jnp.zeros((depth, VEC_ROWS - r, D_MODEL), F32))
    vecs = jnp.concatenate(pieces, axis=1)
    return vecs, w_all, agate, w_branch.astype(MXU_DTYPE), w_out.astype(MXU_DTYPE)


def kernel(x, norm_g, w_in, a_conv_w, a_conv_b, a_gate_a_w, a_gate_a_b, a_gate_x_w, a_gate_x_b, a_lambda,
           b_conv_w, b_conv_b, b_dt_bias, b_a_log, b_d_skip, b_norm_g, c_i_bias, c_f_bias, c_norm_g,
           d_norm_g, w_branch, w_out, final_norm_g):
    depth = w_in.shape[0]
    seq = x.shape[1]
    ts = min(SEQ_TILE, seq)
    lc = min(CHUNK, ts // 2)
    rope_tab, dtab, dmat = _retention_tables(seq, lc)
    ops = _prepare_operands(norm_g, w_in, a_conv_w, a_conv_b, a_gate_a_w, a_gate_a_b, a_gate_x_w, a_gate_x_b,
                            a_lambda, b_conv_w, b_conv_b, b_dt_bias, b_a_log, b_d_skip, b_norm_g, c_i_bias,
                            c_f_bias, c_norm_g, d_norm_g, w_branch, w_out, final_norm_g)
    for l in range(depth):
        x = _layer_call(x, l, *ops, rope_tab, dtab, dmat, ts=ts, lc=lc,
                        first_layer=(l == 0), last_layer=(l == depth - 1))
    return x
```

```python
import functools

import jax
import jax.numpy as jnp
from jax import lax
from jax.experimental import pallas as pl
from jax.experimental.pallas import tpu as pltpu

F32 = jnp.float32
MXU_DTYPE = jnp.bfloat16

D_MODEL = 1024
BR = 512
EPS = 1e-6
CONV_W = 4
RG_BLOCKS = 8
RG_C = 8.0
SSD_HEADS = 8
SSD_GROUPS = 2
SSD_HPG = 4
SSD_HEAD_DIM = 64
SSD_STATE = 128
ML_HEADS = 4
ML_HEAD_DIM = 128
RET_HEADS = 4
RET_QK = 64
RET_V = 128
ROPE_BASE = 10000.0
LANES = 128
SUBLANES = 8

SEQ_TILE = 256
CHUNK = 128
RELAYOUT_ROWS = 128
FILL_AFTER_GATES = 9
FILL_COLS = 512
VMEM_LIMIT_BYTES = 60 * 1024 * 1024

(R_NORM_G, R_FINAL_G, R_A_CONVW, R_A_CONVB, R_A_GATEB, R_A_LAM, R_B_CONVW, R_B_CONVB, R_B_DSKIP,
 R_B_NORMG, R_C_NORMG, R_D_NORMG, R_B_DTBIAS, R_B_ALOG, R_C_IBIAS, R_C_FBIAS) = (
    0, 1, 2, 6, 7, 8, 9, 13, 14, 15, 16, 17, 18, 19, 20, 21)
VEC_ROWS = 24

SMALL_DT, SMALL_I, SMALL_F = 0, 8, 12

OFF_A = 0
OFF_S = OFF_A + 2 * BR
OFF_B = OFF_S + LANES
OFF_C = OFF_B + 3 * BR
OFF_D = OFF_C + 5 * BR
OFF_G = OFF_D + 2 * RET_HEADS * RET_QK + 2 * BR
W_COLS = OFF_G + 4 * D_MODEL


def _mm(a, b):
    return jnp.dot(a.astype(MXU_DTYPE), b.astype(MXU_DTYPE), preferred_element_type=F32)


def _mm_nt(a, b):
    return lax.dot_general(a.astype(MXU_DTYPE), b.astype(MXU_DTYPE), (((1,), (1,)), ((), ())),
                           preferred_element_type=F32)


def _mm_tn(a, b):
    return lax.dot_general(a.astype(MXU_DTYPE), b.astype(MXU_DTYPE), (((0,), (0,)), ((), ())),
                           preferred_element_type=F32)


def _sigmoid(x):
    return 0.5 * jnp.tanh(0.5 * x) + 0.5


def _silu(x):
    return x * _sigmoid(x)


def _softplus(x):
    return jnp.maximum(x, 0.0) + jnp.log1p(jnp.exp(-jnp.abs(x)))


def _rmsnorm(x, g):
    return x * lax.rsqrt(jnp.mean(x * x, axis=-1, keepdims=True) + EPS) * g


def _headwise_rmsnorm(y, g, n_heads):
    w = y.shape[1] // n_heads
    parts = []
    for h in range(n_heads):
        yh = y[:, h * w:(h + 1) * w]
        parts.append(yh * lax.rsqrt(jnp.mean(yh * yh, axis=-1, keepdims=True) + EPS))
    return jnp.concatenate(parts, axis=1) * g


def _to_strided(x, lc):
    nv = lc // SUBLANES
    return jnp.concatenate(
        [jnp.swapaxes(x[c * lc:(c + 1) * lc, :].reshape(SUBLANES, nv, x.shape[1]), 0, 1).reshape(lc, x.shape[1])
         for c in range(x.shape[0] // lc)], axis=0)


def _from_strided(x, lc):
    nv = lc // SUBLANES
    return jnp.concatenate(
        [jnp.swapaxes(x[c * lc:(c + 1) * lc, :].reshape(nv, SUBLANES, x.shape[1]), 0, 1).reshape(lc, x.shape[1])
         for c in range(x.shape[0] // lc)], axis=0)


def _strided_time(shape, axis, lc):
    i = lax.broadcasted_iota(jnp.int32, shape, axis)
    return (i & (SUBLANES - 1)) * (lc // SUBLANES) + (i >> 3)


def _sublane(width):
    return lax.broadcasted_iota(jnp.int32, (SUBLANES, width), 0)


def _slabs(v, r0, n):
    return [v[r0 + i * SUBLANES:r0 + (i + 1) * SUBLANES, :] for i in range(n)]


def _cumsum_time(v, lc):
    nv = lc // SUBLANES
    sub = _sublane(v.shape[1])
    out = []
    for c in range(v.shape[0] // lc):
        acc = _slabs(v, c * lc, nv)
        for i in range(1, nv):
            acc[i] = acc[i - 1] + acc[i]
        tot = acc[-1]
        inc = tot
        for k in (1, 2, 4):
            inc = inc + jnp.where(sub >= k, pltpu.roll(inc, k, 0), 0.0)
        out += [a + (inc - tot) for a in acc]
    return jnp.concatenate(out, axis=0)


def _scan_time(a, d, h_prev, lc):
    nv = lc // SUBLANES
    sub = _sublane(a.shape[1])
    out = []
    for c in range(a.shape[0] // lc):
        p, h = _slabs(a, c * lc, nv), _slabs(d, c * lc, nv)
        for i in range(1, nv):
            h[i] = p[i] * h[i - 1] + h[i]
            p[i] = p[i] * p[i - 1]
        pt, ht = p[-1], h[-1]
        for k in (1, 2, 4):
            keep = sub >= k
            ht = pt * jnp.where(keep, pltpu.roll(ht, k, 0), 0.0) + ht
            pt = pt * jnp.where(keep, pltpu.roll(pt, k, 0), 1.0)
        ends = ht + pt * h_prev
        carry = jnp.where(sub == 0, h_prev, pltpu.roll(ends, 1, 0))
        out += [h[i] + p[i] * carry for i in range(nv)]
        h_prev = ends[SUBLANES - 1:SUBLANES, :]
    return jnp.concatenate(out, axis=0), h_prev


def _chunk_last_bcast(v, lc):
    n = v.shape[0] // lc
    return jnp.concatenate(
        [jnp.broadcast_to(v[(c + 1) * lc - 1:(c + 1) * lc, :], (lc, v.shape[1])) for c in range(n)], axis=0)


def _head_bcast(v, n_heads, head_w):
    r = v.shape[0]
    cols = [jnp.broadcast_to(v[:, h:h + 1], (r, LANES)) for h in range(n_heads)]
    if head_w == LANES:
        return jnp.concatenate(cols, axis=1)
    assert head_w * 2 == LANES
    lane = lax.broadcasted_iota(jnp.int32, (r, LANES), 1)
    return jnp.concatenate(
        [jnp.where(lane < head_w, cols[2 * j], cols[2 * j + 1]) for j in range(n_heads // 2)], axis=1)


def _block_rows(x, n_blocks):
    rows, w = x.shape
    bl = w // n_blocks
    n_tiles = w // LANES
    zero = jnp.zeros((rows, LANES), MXU_DTYPE)
    half = lax.broadcasted_iota(jnp.int32, (rows, LANES), 1) // bl if bl < LANES else None
    out = []
    for h in range(n_blocks):
        tiles = []
        for j in range(n_tiles):
            if bl >= LANES:
                keep = (j * LANES) // bl == h
                tiles.append(x[:, j * LANES:(j + 1) * LANES].astype(MXU_DTYPE) if keep else zero)
            elif j == (h * bl) // LANES:
                tiles.append(jnp.where(half == h % (LANES // bl), x[:, j * LANES:(j + 1) * LANES], 0.0)
                             .astype(MXU_DTYPE))
            else:
                tiles.append(zero)
        out.append(jnp.concatenate(tiles, axis=1))
    return jnp.concatenate(out, axis=0)


def _block_diag(blocks):
    n = len(blocks)
    zero = jnp.zeros(blocks[0].shape, MXU_DTYPE)
    return jnp.concatenate(
        [jnp.concatenate([blocks[i].astype(MXU_DTYPE) if j == i else zero for j in range(n)], axis=1)
         for i in range(n)], axis=0)


def _lane_bcast_heads(cols, width):
    return jnp.concatenate([jnp.broadcast_to(c, (c.shape[0], width)) for c in cols], axis=1)


def _causal_conv(tail_ref, x, w, b, lc):
    nt = (CONV_W - 1) * SUBLANES
    sub0 = _sublane(x.shape[1]) == 0
    prev_tail = tail_ref[...]
    out = []
    for c in range(x.shape[0] // lc):
        xc = x[c * lc:(c + 1) * lc, :]
        cur_tail = xc[lc - nt:lc, :]
        wrap = jnp.concatenate(
            [jnp.where(sub0, pltpu.roll(p, 1, 0), pltpu.roll(q, 1, 0))
             for p, q in zip(_slabs(prev_tail, 0, CONV_W - 1), _slabs(cur_tail, 0, CONV_W - 1))], axis=0)
        acc = b + w[CONV_W - 1:CONV_W, :] * xc
        for sh in range(1, CONV_W):
            shifted = jnp.concatenate([wrap[nt - sh * SUBLANES:nt, :], xc[0:lc - sh * SUBLANES, :]], axis=0)
            acc = acc + w[CONV_W - 1 - sh:CONV_W - sh, :] * shifted
        out.append(acc)
        prev_tail = cur_tail
    tail_ref[...] = prev_tail
    return jnp.concatenate(out, axis=0)


def _layer_kernel(x_ref, vec_ref, w_ref, agate_ref, wbr_ref, wout_ref, rope_ref, dtab_ref, dmat_ref, o_ref,
                  a_tail, a_h, b_tail, b_state, c_state, c_n, c_m, d_state,
                  *, ts, lc, first_layer, last_layer):
    nchunk = ts // lc

    @pl.when(pl.program_id(1) == 0)
    def _():
        a_tail[...] = jnp.zeros_like(a_tail)
        a_h[...] = jnp.zeros_like(a_h)
        b_tail[...] = jnp.zeros_like(b_tail)
        b_state[...] = jnp.zeros_like(b_state)
        c_state[...] = jnp.zeros_like(c_state)
        c_n[...] = jnp.zeros_like(c_n)
        c_m[...] = jnp.zeros_like(c_m)
        d_state[...] = jnp.zeros_like(d_state)

    def vec(row, width, nrows=1):
        return vec_ref[row:row + nrows, 0:width]

    half = ts // 2
    x_halves, hn_halves, pa_halves = [], [], []
    for r0 in (0, half):
        xh_ = x_ref[0, r0:r0 + half, :]
        xh_ = _to_strided(xh_, lc) if first_layer else xh_
        hh_ = _rmsnorm(xh_, vec(R_NORM_G, D_MODEL)).astype(MXU_DTYPE)
        x_halves.append(xh_)
        hn_halves.append(hh_)
        pa_halves.append(_mm(hh_, w_ref[:, OFF_A:OFF_B]))
    hn = jnp.concatenate(hn_halves, axis=0)
    pa = jnp.concatenate(pa_halves, axis=0)
    small = pa[:, 2 * BR:2 * BR + LANES]

    pending, pieces = [], {}

    def defer_projection(key, c0, c1):
        pieces[key] = []
        for a in range(c0, c1, FILL_COLS):
            pending.append((key, a, min(a + FILL_COLS, c1)))

    def fill(n=1):
        for _ in range(min(n, len(pending))):
            key, a, b = pending.pop(0)
            pieces[key].append(_mm(hn, w_ref[:, a:b]))

    def take(key):
        while any(k == key for k, _, _ in pending):
            fill()
        return jnp.concatenate(pieces.pop(key), axis=1)

    defer_projection("b", OFF_B, OFF_C)
    defer_projection("c", OFF_C, OFF_D)
    defer_projection("d", OFF_D, OFF_G)
    for n in range(4):
        defer_projection(f"g{n}", OFF_G + n * D_MODEL, OFF_G + (n + 1) * D_MODEL)
    branch_out = {}

    def gated_up(n, y):
        return _sigmoid(take(f"g{n}")) * _mm(y, wbr_ref[n])

    causal = _strided_time((lc, lc), 0, lc) >= _strided_time((lc, lc), 1, lc)
    neg_inf = jnp.float32(-jnp.inf)

    u, za = pa[:, 0:BR], pa[:, BR:2 * BR]
    xc = _causal_conv(a_tail, u, vec(R_A_CONVW, BR, CONV_W), vec(R_A_CONVB, BR), lc)
    hw = BR // 2
    fill(2)
    g_halves = [_mm(xc[:, j * hw:(j + 1) * hw], agate_ref[j]) for j in range(2)]
    fill(FILL_AFTER_GATES)
    gate_b = vec(R_A_GATEB, 2 * BR)
    r_gate = _sigmoid(jnp.concatenate([g[:, 0:hw] for g in g_halves], axis=1) + gate_b[:, 0:BR])
    i_gate = _sigmoid(jnp.concatenate([g[:, hw:2 * hw] for g in g_halves], axis=1) + gate_b[:, BR:2 * BR])
    log_a = (-RG_C) * r_gate * _softplus(-vec(R_A_LAM, BR))
    a = jnp.exp(log_a)
    drive = jnp.sqrt(1.0 - a * a) * (i_gate * xc)
    h, h_last = _scan_time(a, drive, a_h[0:1, :], lc)
    a_h[0:1, :] = h_last
    merged = gated_up(0, h * _silu(za))

    def ssd_mixer():
        pb = take("b")
        zb = pb[:, 2 * BR:3 * BR]
        xbc = _silu(_causal_conv(b_tail, pb[:, 0:2 * BR], vec(R_B_CONVW, 2 * BR, CONV_W),
                                 vec(R_B_CONVB, 2 * BR), lc))
        xh = xbc[:, 0:BR]
        bm = xbc[:, BR:BR + 2 * SSD_STATE]
        cm = xbc[:, BR + 2 * SSD_STATE:BR + 4 * SSD_STATE]
        dt = _softplus(small + vec(R_B_DTBIAS, LANES))
        a_dt = -jnp.exp(vec(R_B_ALOG, LANES)) * dt
        a_cs = _cumsum_time(a_dt, lc)
        a_tot = _chunk_last_bcast(a_cs, lc)
        dt_f = _head_bcast(dt, SSD_HEADS, SSD_HEAD_DIM)
        eacs_f = _head_bcast(jnp.exp(a_cs), SSD_HEADS, SSD_HEAD_DIM)
        dte_f = _head_bcast(jnp.exp(a_tot - a_cs), SSD_HEADS, SSD_HEAD_DIM)
        x_dt = xh * dt_f
        xs = x_dt * dte_f
        gw = SSD_HPG * SSD_HEAD_DIM
        y_rows = []
        for c in range(nchunk):
            r0, r1 = c * lc, (c + 1) * lc
            yield
            acs_c = a_cs[r0:r1, :]
            acs_t = acs_c.T
            c_c, b_c = cm[r0:r1, :], bm[r0:r1, :]
            cb = _mm_nt(c_c, _block_rows(b_c, SSD_GROUPS))
            yield
            y_groups = []
            for g in range(SSD_GROUPS):
                cb_g = cb[:, g * lc:(g + 1) * lc]
                masked = []
                for e in range(SSD_HPG):
                    hh = g * SSD_HPG + e
                    diff = acs_c[:, hh:hh + 1] - acs_t[hh:hh + 1, :]
                    decay = jnp.exp(jnp.where(causal, diff, neg_inf))
                    masked.append((cb_g * decay).astype(MXU_DTYPE))
                y_groups.append(_mm(jnp.concatenate(masked, axis=1),
                                    _block_rows(x_dt[r0:r1, g * gw:(g + 1) * gw], SSD_HPG)))
            states = [b_state[g] for g in range(SSD_GROUPS)]
            y_off = _mm(c_c, _block_diag(states)) * eacs_f[r0:r1, :]
            y_rows.append(jnp.concatenate(y_groups, axis=1) + y_off)
            for g in range(SSD_GROUPS):
                b_state[g] = (eacs_f[r1 - 1:r1, g * gw:(g + 1) * gw] * states[g]
                              + _mm_tn(b_c[:, g * SSD_STATE:(g + 1) * SSD_STATE], xs[r0:r1, g * gw:(g + 1) * gw]))
        y_b = jnp.concatenate(y_rows, axis=0) + vec(R_B_DSKIP, BR) * xh
        branch_out[1] = _rmsnorm(y_b * _silu(zb), vec(R_B_NORMG, BR))

    def mlstm_mixer():
        pc = take("c")
        qc = pc[:, 0:BR]
        kc = pc[:, BR:2 * BR] * (ML_HEAD_DIM ** -0.5)
        vc = pc[:, 2 * BR:3 * BR]
        oc = pc[:, 3 * BR:4 * BR]
        zc = pc[:, 4 * BR:5 * BR]
        log_i = pltpu.roll(small, LANES - SMALL_I, 1) + vec(R_C_IBIAS, LANES)
        log_f = -_softplus(-(pltpu.roll(small, LANES - SMALL_F, 1) + vec(R_C_FBIAS, LANES)))
        f_cs = _cumsum_time(log_f, lc)
        f_tot = _chunk_last_bcast(f_cs, lc)
        w_end = f_tot - f_cs + log_i
        g_src = log_i - f_cs
        m_state = c_m[0:1, :]
        h_rows = []
        for c in range(nchunk):
            r0, r1 = c * lc, (c + 1) * lc
            yield
            fcs_c = f_cs[r0:r1, :]
            w_c = w_end[r0:r1, :]
            m_loc = jnp.max(w_c, axis=0, keepdims=True)
            p_end = jnp.exp(w_c - m_loc)
            g_t = g_src[r0:r1, :].T
            ft = f_tot[r1 - 1:r1, :]
            m_new = jnp.maximum(ft + m_state, m_loc)
            s_old = jnp.exp(ft + m_state - m_new)
            s_loc = jnp.exp(m_loc - m_new)
            q_c, k_c, v_c = qc[r0:r1, :], kc[r0:r1, :], vc[r0:r1, :]
            pairs = [slice(p * 2 * ML_HEAD_DIM, (p + 1) * 2 * ML_HEAD_DIM) for p in range(ML_HEADS // 2)]
            qk = jnp.concatenate([_mm_nt(q_c[:, pr], _block_rows(k_c[:, pr], 2)) for pr in pairs],
                                 axis=1)
            yield
            n_prev = c_n[0:1, :]
            qn = q_c * n_prev
            scores, scale_prev, inv = [], [], []
            for hh in range(ML_HEADS):
                fcol = fcs_c[:, hh:hh + 1]
                log_d = jnp.where(causal, fcol + g_t[hh:hh + 1, :], neg_inf)
                log_prev = fcol + m_state[:, hh:hh + 1]
                m_row = jnp.maximum(log_prev, jnp.max(log_d, axis=1, keepdims=True))
                s_h = qk[:, hh * lc:(hh + 1) * lc] * jnp.exp(log_d - m_row)
                sp = jnp.exp(log_prev - m_row)
                den = (jnp.sum(s_h, axis=1, keepdims=True)
                       + sp * jnp.sum(qn[:, hh * ML_HEAD_DIM:(hh + 1) * ML_HEAD_DIM], axis=1, keepdims=True))
                scores.append(s_h.astype(MXU_DTYPE))
                scale_prev.append(sp)
                inv.append(1.0 / jnp.maximum(jnp.abs(den), jnp.exp(-m_row)))
            c_prev = [c_state[hh] for hh in range(ML_HEADS)]
            intra = jnp.concatenate(
                [_mm(jnp.concatenate(scores[2 * p:2 * p + 2], axis=1), _block_rows(v_c[:, pr], 2))
                 for p, pr in enumerate(pairs)], axis=1)
            inter = jnp.concatenate(
                [_mm_nt(q_c[:, pr], _block_diag(c_prev[2 * p:2 * p + 2])) for p, pr in enumerate(pairs)], axis=1)
            num = intra + _lane_bcast_heads(scale_prev, ML_HEAD_DIM) * inter
            h_rows.append(num * _lane_bcast_heads(inv, ML_HEAD_DIM))
            p_f = _lane_bcast_heads([p_end[:, hh:hh + 1] for hh in range(ML_HEADS)], ML_HEAD_DIM)
            pv = p_f * v_c
            for p, pr in enumerate(pairs):
                c_loc = _mm_tn(pv[:, pr], k_c[:, pr])
                for e in range(2):
                    hh = 2 * p + e
                    d0, d1 = e * ML_HEAD_DIM, (e + 1) * ML_HEAD_DIM
                    c_state[hh] = s_old[:, hh:hh + 1] * c_prev[hh] + s_loc[:, hh:hh + 1] * c_loc[d0:d1, d0:d1]
            so_f = _lane_bcast_heads([s_old[:, hh:hh + 1] for hh in range(ML_HEADS)], ML_HEAD_DIM)
            sl_f = _lane_bcast_heads([s_loc[:, hh:hh + 1] for hh in range(ML_HEADS)], ML_HEAD_DIM)
            c_n[0:1, :] = so_f * n_prev + sl_f * jnp.sum(p_f * k_c, axis=0, keepdims=True)
            m_state = m_new
        c_m[0:1, :] = m_state
        y_c = _sigmoid(oc) * jnp.concatenate(h_rows, axis=0)
        branch_out[2] = _headwise_rmsnorm(y_c, vec(R_C_NORMG, BR), ML_HEADS) * _silu(zc)

    def retention_mixer():
        pd = take("d")
        qk_w = RET_HEADS * RET_QK
        vd = pd[:, 2 * qk_w:2 * qk_w + BR]
        zd = pd[:, 2 * qk_w + BR:2 * qk_w + 2 * BR]
        tab = rope_ref[...]
        swapped = pltpu.roll(tab, RET_QK, 1)
        low = lax.broadcasted_iota(jnp.int32, (ts, LANES), 1) < RET_QK
        cos2 = jnp.concatenate([jnp.where(low, tab, swapped)] * 2, axis=1)
        sin2 = jnp.concatenate([jnp.where(low, swapped, tab)] * 2, axis=1)
        lane_q = lax.broadcasted_iota(jnp.int32, (ts, qk_w), 1)
        first_half = (lane_q & (RET_QK - 1)) < RET_QK // 2

        def rope(t):
            partner = jnp.where(first_half, pltpu.roll(t, qk_w - RET_QK // 2, 1), pltpu.roll(t, RET_QK // 2, 1))
            return t * cos2 + partner * sin2

        qr = rope(pd[:, 0:qk_w])
        kr = rope(pd[:, qk_w:2 * qk_w]) * (RET_QK ** -0.5)
        dec_start = dtab_ref[0:lc, 0:qk_w]
        dec_end = dtab_ref[0:lc, qk_w:2 * qk_w]
        y_rows = []
        for c in range(nchunk):
            r0, r1 = c * lc, (c + 1) * lc
            yield
            qr_c, kr_c, v_c = qr[r0:r1, :], kr[r0:r1, :], vd[r0:r1, :]
            inner = _mm_nt(qr_c, _block_rows(kr_c, RET_HEADS)) * dmat_ref[...]
            yield
            r_prev = [d_state[hh] for hh in range(RET_HEADS)]
            intra = jnp.concatenate(
                [_mm(inner[:, p * 2 * lc:(p + 1) * 2 * lc], _block_rows(v_c[:, p * 2 * RET_V:(p + 1) * 2 * RET_V], 2))
                 for p in range(RET_HEADS // 2)], axis=1)
            y_rows.append(intra + _mm(qr_c * dec_start, _block_diag(r_prev)))
            kd_c = kr_c * dec_end
            for p in range(RET_HEADS // 2):
                upd = _mm_tn(kd_c[:, p * 2 * RET_QK:(p + 1) * 2 * RET_QK],
                             v_c[:, p * 2 * RET_V:(p + 1) * 2 * RET_V])
                for e in range(2):
                    hh = 2 * p + e
                    chunk_dec = dtab_ref[lc + hh:lc + hh + 1, 0:RET_V]
                    d_state[hh] = (chunk_dec * r_prev[hh]
                                   + upd[e * RET_QK:(e + 1) * RET_QK, e * RET_V:(e + 1) * RET_V])
        y_d = jnp.concatenate(y_rows, axis=0)
        branch_out[3] = _headwise_rmsnorm(y_d, vec(R_D_NORMG, BR), RET_HEADS) * _silu(zd)

    mixers = [mlstm_mixer(), ssd_mixer(), retention_mixer()]
    while mixers:
        for m in list(mixers):
            if next(m, "done") == "done":
                mixers.remove(m)
        fill()
    for n in (1, 2, 3):
        merged = merged + gated_up(n, branch_out[n])
    for r0, x_half in zip((0, half), x_halves):
        out = x_half + _mm(merged[r0:r0 + half, :], wout_ref[...])
        if last_layer:
            out = _from_strided(_rmsnorm(out, vec(R_FINAL_G, D_MODEL)), lc)
        o_ref[0, r0:r0 + half, :] = out


def _const_spec(shape):
    zeros = (0,) * len(shape)
    return pl.BlockSpec(shape, lambda b, s: zeros, pipeline_mode=pl.Buffered(1))


def _layer_spec(arr, layer):
    idx = (layer,) + (0,) * (arr.ndim - 1)
    return pl.BlockSpec((None,) + arr.shape[1:], lambda b, s: idx, pipeline_mode=pl.Buffered(1))


def _layer_call(x, layer, vecs, w_all, agate, wbr, wout, rope_tab, dtab, dmat,
                *, ts, lc, first_layer, last_layer):
    bsz, seq, d = x.shape
    assert d == D_MODEL and seq % ts == 0 and ts % (2 * lc) == 0 and lc % LANES == 0
    kern = functools.partial(_layer_kernel, ts=ts, lc=lc, first_layer=first_layer, last_layer=last_layer)
    consts = (vecs, w_all, agate, wbr, wout)
    in_specs = ([pl.BlockSpec((1, ts, d), lambda b, s: (b, s, 0))]
                + [_layer_spec(c, layer) for c in consts]
                + [pl.BlockSpec((ts, LANES), lambda b, s: (s, 0)),
                   _const_spec(dtab.shape), _const_spec(dmat.shape)])
    scratch = [
        pltpu.VMEM(((CONV_W - 1) * SUBLANES, BR), F32),
        pltpu.VMEM((SUBLANES, BR), F32),
        pltpu.VMEM(((CONV_W - 1) * SUBLANES, 2 * BR), F32),
        pltpu.VMEM((SSD_GROUPS, SSD_STATE, SSD_HPG * SSD_HEAD_DIM), F32),
        pltpu.VMEM((ML_HEADS, ML_HEAD_DIM, ML_HEAD_DIM), F32),
        pltpu.VMEM((SUBLANES, BR), F32),
        pltpu.VMEM((SUBLANES, LANES), F32),
        pltpu.VMEM((RET_HEADS, RET_QK, RET_V), F32),
    ]
    return pl.pallas_call(
        kern,
        grid=(bsz, seq // ts),
        in_specs=in_specs,
        out_specs=pl.BlockSpec((1, ts, d), lambda b, s: (b, s, 0)),
        out_shape=jax.ShapeDtypeStruct(x.shape, x.dtype),
        scratch_shapes=scratch,
        compiler_params=pltpu.CompilerParams(
            dimension_semantics=("arbitrary", "arbitrary"),
            vmem_limit_bytes=VMEM_LIMIT_BYTES),
        name="hybrid_layer",
    )(x, *consts, rope_tab, dtab, dmat)


def _retention_tables(seq, lc):
    half = RET_QK // 2
    nv = lc // SUBLANES
    inv = ROPE_BASE ** (-jnp.arange(half, dtype=F32) / half)
    row = jnp.arange(seq)
    pos = (row // lc) * lc + (row % SUBLANES) * nv + (row % lc) // SUBLANES
    ang = pos.astype(F32)[:, None] * inv
    cos, sin = jnp.cos(ang), jnp.sin(ang)
    rope_tab = jnp.concatenate([cos, cos, -sin, sin], axis=1)
    log_g = jnp.log1p(-jnp.exp2(-5.0 - jnp.arange(RET_HEADS, dtype=F32)))
    idx = pos[:lc].astype(F32)
    rel = idx[:, None] - idx[None, :]
    dmat = jnp.where(rel >= 0, jnp.exp(log_g[:, None, None] * jnp.maximum(rel, 0.0)), 0.0)
    dmat = dmat.transpose(1, 0, 2).reshape(lc, RET_HEADS * lc)
    dec_start = jnp.repeat(jnp.exp(log_g[None, :] * (idx[:, None] + 1.0)), RET_QK, axis=1)
    dec_end = jnp.repeat(jnp.exp(log_g[None, :] * (lc - 1.0 - idx[:, None])), RET_QK, axis=1)
    chunk_dec = jnp.broadcast_to(jnp.exp(log_g * lc)[:, None], (RET_HEADS, 2 * RET_HEADS * RET_QK))
    dtab = jnp.concatenate([jnp.concatenate([dec_start, dec_end], axis=1), chunk_dec,
                            jnp.zeros((SUBLANES - RET_HEADS, 2 * RET_HEADS * RET_QK), F32)], axis=0)
    return rope_tab, dtab, dmat


def _relayout_kernel(wt_ref, o_ref, *, segments):
    narrow = [s for s in segments if s is None or s[1] - s[0] < LANES]
    parts = [wt_ref[s[0]:s[1], :] for s in narrow if s is not None]
    used = sum(p.shape[0] for p in parts)
    parts.append(jnp.zeros((LANES - used, wt_ref.shape[1]), F32))
    narrow_tile = jnp.concatenate(parts, axis=0).T.astype(o_ref.dtype)
    off, narrow_done = 0, False
    for seg in segments:
        if seg in narrow:
            if not narrow_done:
                o_ref[:, off:off + LANES] = narrow_tile
                off += LANES
                narrow_done = True
            continue
        width = seg[1] - seg[0]
        for c0 in range(0, width, FILL_COLS):
            c1 = min(c0 + FILL_COLS, width)
            o_ref[:, off + c0:off + c1] = wt_ref[seg[0] + c0:seg[0] + c1, :].T.astype(o_ref.dtype)
        off += width
    assert off == W_COLS


def _relayout_projection(w_in, segments):
    depth, d, w = w_in.shape
    rows = RELAYOUT_ROWS
    assert d % rows == 0
    return pl.pallas_call(
        functools.partial(_relayout_kernel, segments=segments),
        grid=(depth, d // rows),
        in_specs=[pl.BlockSpec((None, w, rows), lambda l, r: (l, 0, r))],
        out_specs=pl.BlockSpec((None, rows, W_COLS), lambda l, r: (l, r, 0)),
        out_shape=jax.ShapeDtypeStruct((depth, d, W_COLS), MXU_DTYPE),
        compiler_params=pltpu.CompilerParams(dimension_semantics=("arbitrary", "arbitrary"),
                                             vmem_limit_bytes=VMEM_LIMIT_BYTES),
        name="relayout_projection",
    )(jnp.swapaxes(w_in, 1, 2))


def _prepare_operands(norm_g, w_in, a_conv_w, a_conv_b, a_gate_a_w, a_gate_a_b, a_gate_x_w, a_gate_x_b,
                      a_lambda, b_conv_w, b_conv_b, b_dt_bias, b_a_log, b_d_skip, b_norm_g, c_i_bias,
                      c_f_bias, c_norm_g, d_norm_g, w_branch, w_out, final_norm_g):
    depth = w_in.shape[0]
    o = 0
    a0 = o; o += 2 * BR
    b0 = o; o += 3 * BR
    dt0 = o; o += SSD_HEADS
    c0 = o; o += 5 * BR
    if0 = o; o += 2 * ML_HEADS
    d0 = o; o += 2 * RET_HEADS * RET_QK + 2 * BR
    g0 = o; o += 4 * D_MODEL
    assert o == w_in.shape[2]
    segments = ((a0, b0), (dt0, c0), (if0, d0), None, (b0, dt0), (c0, if0), (d0, o))
    w_all = _relayout_projection(w_in, segments)

    def gate_blocks(t):
        per_half = RG_BLOCKS // 2
        blk = t.shape[-1]
        rows = [jnp.pad(t[:, h], ((0, 0), (0, 0), ((h % per_half) * blk, (per_half - 1 - h % per_half) * blk)))
                for h in range(RG_BLOCKS)]
        halves = [jnp.concatenate(rows[j * per_half:(j + 1) * per_half], axis=1) for j in range(2)]
        return jnp.stack(halves, axis=1)

    agate = jnp.concatenate([gate_blocks(a_gate_a_w), gate_blocks(a_gate_x_w)], axis=3).astype(MXU_DTYPE)

    def row(v, width=D_MODEL):
        return jnp.pad(v.astype(F32), ((0, 0), (0, width - v.shape[1])))[:, None, :]

    def rows(v):
        return jnp.pad(v.astype(F32), ((0, 0), (0, 0), (0, D_MODEL - v.shape[2])))

    table = {
        R_NORM_G: row(norm_g),
        R_FINAL_G: row(jnp.broadcast_to(final_norm_g, (depth, D_MODEL))),
        R_A_CONVW: rows(a_conv_w),
        R_A_CONVB: row(a_conv_b),
        R_A_GATEB: row(jnp.concatenate([a_gate_a_b, a_gate_x_b], axis=1)),
        R_A_LAM: row(a_lambda),
        R_B_CONVW: rows(b_conv_w),
        R_B_CONVB: row(b_conv_b),
        R_B_DSKIP: row(jnp.repeat(b_d_skip, SSD_HEAD_DIM, axis=1)),
        R_B_NORMG: row(b_norm_g),
        R_C_NORMG: row(c_norm_g),
        R_D_NORMG: row(d_norm_g),
        R_B_DTBIAS: row(b_dt_bias),
        R_B_ALOG: row(b_a_log),
        R_C_IBIAS: row(c_i_bias),
        R_C_FBIAS: row(c_f_bias),
    }
    pieces, r = [], 0
    for start in sorted(table):
        assert start == r
        pieces.append(table[start])
        r += table[start].shape[1]
    pieces.append(jnp.zeros((depth, VEC_ROWS - r, D_MODEL), F32))
    vecs = jnp.concatenate(pieces, axis=1)
    return vecs, w_all, agate, w_branch.astype(MXU_DTYPE), w_out.astype(MXU_DTYPE)


def kernel(x, norm_g, w_in, a_conv_w, a_conv_b, a_gate_a_w, a_gate_a_b, a_gate_x_w, a_gate_x_b, a_lambda,
           b_conv_w, b_conv_b, b_dt_bias, b_a_log, b_d_skip, b_norm_g, c_i_bias, c_f_bias, c_norm_g,
           d_norm_g, w_branch, w_out, final_norm_g):
    depth = w_in.shape[0]
    seq = x.shape[1]
    ts = min(SEQ_TILE, seq)
    lc = min(CHUNK, ts // 2)
    rope_tab, dtab, dmat = _retention_tables(seq, lc)
    ops = _prepare_operands(norm_g, w_in, a_conv_w, a_conv_b, a_gate_a_w, a_gate_a_b, a_gate_x_w, a_gate_x_b,
                            a_lambda, b_conv_w, b_conv_b, b_dt_bias, b_a_log, b_d_skip, b_norm_g, c_i_bias,
                            c_f_bias, c_norm_g, d_norm_g, w_branch, w_out, final_norm_g)
    for l in range(depth):
        x = _layer_call(x, l, *ops, rope_tab, dtab, dmat, ts=ts, lc=lc,
                        first_layer=(l == 0), last_layer=(l == depth - 1))
    return x
```
